```python
import math
import jax, jax.numpy as jnp
from jax import lax
import numpy as np

D_MODEL = 1024
BATCH = 8
SEQ = 2048
DEPTH = 1
DEC_BATCH = 32
DEC_SEQ = 1
PAST_LEN = 8192
PAGE_SIZE = 128

HEAD_DIM = 64
SB_HEADS = 8
NSA_HEADS = 8
NSA_KV_HEADS = 2
NSA_GROUP = NSA_HEADS // NSA_KV_HEADS
SB_WIDTH = SB_HEADS * HEAD_DIM
NSA_WIDTH = NSA_HEADS * HEAD_DIM
MIX_WIDTH = SB_WIDTH + NSA_WIDTH
KV_WIDTH = NSA_KV_HEADS * HEAD_DIM
IN_COLS = 3 * SB_WIDTH + NSA_WIDTH + 3 * 2 * KV_WIDTH + 3 * NSA_HEADS
CMP_STRIDE = 16
CMP_RATIO = 2
CMP_BLK = CMP_STRIDE * CMP_RATIO
CMP_HID = 2 * HEAD_DIM
SEL_BLK = 64
TOP_N = 16
N_LOCAL = 2
WINDOW = 512
D_FF = 4 * D_MODEL
Q_BLK = 128
ROPE_THETA = 10000.0
EPS = 1e-6
NEG = -1e30
TINY = 1e-30
FORCE = 1e9

kernel_name = "hymba_stickbreak_nsa_decoder_step"


def rmsnorm(x, g):
    xf = x.astype(jnp.float32)
    y = xf * lax.rsqrt(jnp.mean(xf * xf, axis=-1, keepdims=True) + EPS)
    return (y * g.astype(jnp.float32)).astype(x.dtype)


def rope(x, pos):
    half = HEAD_DIM // 2
    inv = ROPE_THETA ** (-jnp.arange(half, dtype=jnp.float32) / half)
    ang = pos.astype(jnp.float32)[:, None] * inv[None, :]
    cos = jnp.cos(ang)[None, :, None, :]
    sin = jnp.sin(ang)[None, :, None, :]
    xf = x.astype(jnp.float32)
    x1, x2 = xf[..., :half], xf[..., half:]
    return jnp.concatenate([x1 * cos - x2 * sin, x2 * cos + x1 * sin], axis=-1).astype(x.dtype)


def masked_probs(s, mask):
    s = jnp.where(mask, s, NEG)
    m = jnp.max(s, axis=-1, keepdims=True)
    p = jnp.exp(s - m) * mask
    return p / jnp.maximum(jnp.sum(p, axis=-1, keepdims=True), TINY)


def project(h, w_in, pos):
    B, T, _ = h.shape
    z = h @ w_in
    sizes = (SB_WIDTH, SB_WIDTH, SB_WIDTH, NSA_WIDTH, 2 * KV_WIDTH, 2 * KV_WIDTH, 2 * KV_WIDTH, 3 * NSA_HEADS)
    offs = [int(o) for o in np.cumsum(sizes)[:-1]]
    q_sb, k_sb, v_sb, q_n, kv_c, kv_s, kv_w, g = jnp.split(z, offs, axis=-1)

    def heads(t, n):
        return t.reshape(B, T, n, HEAD_DIM)

    def kv_pair(t):
        t = t.reshape(B, T, 2, NSA_KV_HEADS, HEAD_DIM)
        return jnp.stack([rope(t[:, :, 0], pos), t[:, :, 1]], axis=2)

    gates = jax.nn.sigmoid(g.astype(jnp.float32)).reshape(B, T, NSA_HEADS, 3)
    return (heads(q_sb, SB_HEADS), heads(k_sb, SB_HEADS), heads(v_sb, SB_HEADS),
            rope(heads(q_n, NSA_HEADS), pos), kv_pair(kv_c), kv_pair(kv_s), kv_pair(kv_w), gates)


def sb_attend(q, k, v, q_pos, k_pos):
    z = jnp.einsum('bqhd,bshd->bhqs', q.astype(jnp.float32), k.astype(jnp.float32)) / math.sqrt(HEAD_DIM)
    mask = k_pos[None, :] < q_pos[:, None]
    log_1m = jnp.where(mask, jax.nn.log_sigmoid(-z), 0.0)
    after = lax.cumsum(log_1m, axis=3, reverse=True) - log_1m
    a = jnp.where(mask, jnp.exp(jax.nn.log_sigmoid(z) + after), 0.0)
    return jnp.einsum('bhqs,bshd->bqhd', a, v.astype(jnp.float32)).astype(q.dtype)


def unblock(o):
    nb, B, Q, H, dh = o.shape
    return o.transpose(1, 0, 2, 3, 4).reshape(B, nb * Q, H, dh)


def sb_prompt(q, k, v):
    T = q.shape[1]
    pos = jnp.arange(T)

    def blk(i):
        qs = i * Q_BLK
        qb = lax.dynamic_slice_in_dim(q, qs, Q_BLK, axis=1)
        return sb_attend(qb, k, v, qs + jnp.arange(Q_BLK), pos)

    return unblock(lax.map(blk, jnp.arange(T // Q_BLK)))


def nsa_compress(kv, w_ck1, w_ck2, w_cv1, w_cv2, pe_k, pe_v):
    B, N = kv.shape[:2]
    n_chunks = N // CMP_STRIDE
    n_cmp = n_chunks - CMP_RATIO + 1
    ch = kv[:, :n_chunks * CMP_STRIDE].reshape(B, n_chunks, CMP_STRIDE, 2, NSA_KV_HEADS, HEAD_DIM)
    blocks = jnp.concatenate([ch[:, j:j + n_cmp] for j in range(CMP_RATIO)], axis=2)

    def mlp(x, pe, w1, w2):
        x = (x + pe[None, None, :, None, :]).transpose(0, 3, 1, 2, 4).reshape(B, NSA_KV_HEADS, n_cmp, CMP_BLK * HEAD_DIM)
        return jax.nn.silu(x @ w1) @ w2

    kc = mlp(blocks[:, :, :, 0], pe_k, w_ck1, w_ck2)
    vc = mlp(blocks[:, :, :, 1], pe_v, w_cv1, w_cv2)
    ends = jnp.arange(n_cmp) * CMP_STRIDE + CMP_BLK - 1
    return kc, vc, ends


def nsa_prepare(kv_cmp, kv_slc, cw):
    kc, vc, c_end = nsa_compress(kv_cmp, *cw)
    B, N = kv_slc.shape[:2]
    n_sel = -(-N // SEL_BLK)
    pad = n_sel * SEL_BLK - N
    kvs = jnp.pad(kv_slc, ((0, 0), (0, pad), (0, 0), (0, 0), (0, 0)))
    kvs = kvs.reshape(B, n_sel, SEL_BLK, 2, NSA_KV_HEADS, HEAD_DIM).transpose(3, 0, 4, 1, 2, 5)
    c_start = jnp.arange(kc.shape[2]) * CMP_STRIDE
    s_start = jnp.arange(n_sel) * SEL_BLK
    overlap = ((c_start[:, None] < s_start[None, :] + SEL_BLK)
               & (c_start[:, None] + CMP_BLK > s_start[None, :])).astype(jnp.float32)
    return kc, vc, c_end, kvs[0], kvs[1], overlap


def group_q(q):
    B, Q = q.shape[:2]
    return q.astype(jnp.float32).reshape(B, Q, NSA_KV_HEADS, NSA_GROUP, HEAD_DIM).transpose(0, 2, 3, 1, 4)


def ungroup(o):
    B, K, G, Q, dh = o.shape
    return o.transpose(0, 3, 1, 2, 4).reshape(B, Q, K * G, dh)


def nsa_cmp_slc(q, q_pos, kc, vc, c_end, k_sel, v_sel, overlap):
    B = q.shape[0]
    qg = group_q(q)
    scale = HEAD_DIM ** -0.5
    s_c = jnp.einsum('bkgqd,bkcd->bkgqc', qg, kc.astype(jnp.float32)) * scale
    p_c = masked_probs(s_c, c_end[None, :] <= q_pos[:, None])
    o_c = jnp.einsum('bkgqc,bkcd->bkgqd', p_c, vc.astype(jnp.float32))
    n_sel = k_sel.shape[2]
    imp = jnp.einsum('bkgqc,cj->bkqj', p_c, overlap)
    blk = jnp.arange(n_sel)[None, :]
    cur = (q_pos // SEL_BLK)[:, None]
    forced = (blk == 0) | ((blk <= cur) & (blk > cur - N_LOCAL))
    imp = jnp.where(forced, FORCE, imp)
    imp = jnp.where(blk * SEL_BLK <= q_pos[:, None], imp, NEG)
    _, idx = lax.top_k(imp, min(TOP_N, n_sel))
    bi = jnp.arange(B)[:, None, None, None]
    ki = jnp.arange(NSA_KV_HEADS)[None, :, None, None]
    kg = k_sel[bi, ki, idx].astype(jnp.float32)
    vg = v_sel[bi, ki, idx].astype(jnp.float32)
    s_s = jnp.einsum('bkgqd,bkqnld->bkgqnl', qg, kg) * scale
    key_pos = idx[..., None] * SEL_BLK + jnp.arange(SEL_BLK)
    m_s = key_pos <= q_pos[None, None, :, None, None]
    sh = s_s.shape
    p_s = masked_probs(s_s.reshape(sh[:4] + (-1,)), m_s[:, :, None].reshape(sh[0], sh[1], 1, sh[3], -1)).reshape(sh)
    o_s = jnp.einsum('bkgqnl,bkqnld->bkgqd', p_s, vg)
    return ungroup(o_c), ungroup(o_s)


def win_attend(q, q_pos, kv, k_pos):
    qg = group_q(q)
    k = kv[:, :, 0].astype(jnp.float32)
    v = kv[:, :, 1].astype(jnp.float32)
    s = jnp.einsum('bkgqd,bskd->bkgqs', qg, k) * HEAD_DIM ** -0.5
    kp, qp = k_pos[None, :], q_pos[:, None]
    m = (kp <= qp) & (kp > qp - WINDOW) & (kp >= 0)
    return ungroup(jnp.einsum('bkgqs,bskd->bkgqd', masked_probs(s, m), v))


def nsa_queries(q, q_pos, gates, prep, win_kv, win_pos):
    o_c, o_s = nsa_cmp_slc(q, q_pos, *prep)
    o_w = win_attend(q, q_pos, win_kv, win_pos)
    o = gates[..., 0:1] * o_c + gates[..., 1:2] * o_s + gates[..., 2:3] * o_w
    return o.astype(q.dtype)


def mixer_out(o_sb, o_n, g_sb, g_nsa, w_out):
    B, T = o_sb.shape[:2]
    a = rmsnorm(o_sb.reshape(B, T, SB_WIDTH), g_sb)
    b = rmsnorm(o_n.reshape(B, T, NSA_WIDTH), g_nsa)
    return jnp.concatenate([a, b], axis=-1) @ w_out


def mixer_prompt(h, w_in, cw, g_sb, g_nsa, w_out):
    B, T, _ = h.shape
    pos = jnp.arange(T)
    q_sb, k_sb, v_sb, q_n, kv_c, kv_s, kv_w, gates = project(h, w_in, pos)
    o_sb = sb_prompt(q_sb, k_sb, v_sb)
    prep = nsa_prepare(kv_c, kv_s, cw)
    kv_w_pad = jnp.pad(kv_w, ((0, 0), (WINDOW, 0), (0, 0), (0, 0), (0, 0)))

    def blk(i):
        qs = i * Q_BLK
        qb = lax.dynamic_slice_in_dim(q_n, qs, Q_BLK, axis=1)
        gb = lax.dynamic_slice_in_dim(gates, qs, Q_BLK, axis=1)
        wkv = lax.dynamic_slice_in_dim(kv_w_pad, qs, WINDOW + Q_BLK, axis=1)
        wpos = qs - WINDOW + jnp.arange(WINDOW + Q_BLK)
        return nsa_queries(qb, qs + jnp.arange(Q_BLK), gb, prep, wkv, wpos)

    o_n = unblock(lax.map(blk, jnp.arange(T // Q_BLK)))
    out = mixer_out(o_sb, o_n, g_sb, g_nsa, w_out)
    return out, jnp.stack([k_sb, v_sb], axis=2), kv_c, kv_s, kv_w[:, T - min(WINDOW, T):]


def gather_pages(cache, page_table):
    pages = cache[page_table]
    B, NP, PS = pages.shape[:3]
    return pages.reshape((B, NP * PS) + pages.shape[3:])


def mixer_sample(h, cache_sb, cache_cmp, cache_slc, win_buf, page_table, w_in, cw, g_sb, g_nsa, w_out):
    B, T, _ = h.shape
    past = page_table.shape[1] * cache_sb.shape[1]
    pos = past + jnp.arange(T)
    q_sb, k_sb, v_sb, q_n, kv_c, kv_s, kv_w, gates = project(h, w_in, pos)
    sb_past = gather_pages(cache_sb, page_table)
    k_all = jnp.concatenate([sb_past[:, :, 0], k_sb], axis=1)
    v_all = jnp.concatenate([sb_past[:, :, 1], v_sb], axis=1)
    o_sb = sb_attend(q_sb, k_all, v_all, pos, jnp.arange(past + T))
    kvc_all = jnp.concatenate([gather_pages(cache_cmp, page_table), kv_c], axis=1)
    kvs_all = jnp.concatenate([gather_pages(cache_slc, page_table), kv_s], axis=1)
    prep = nsa_prepare(kvc_all, kvs_all, cw)
    W = win_buf.shape[1]
    win_all = jnp.concatenate([win_buf, kv_w], axis=1)
    win_pos = past - W + jnp.arange(W + T)
    o_n = nsa_queries(q_n, pos, gates, prep, win_all, win_pos)
    out = mixer_out(o_sb, o_n, g_sb, g_nsa, w_out)
    n_keep = min(WINDOW, W + T)
    return out, jnp.stack([k_sb, v_sb], axis=2), kv_c, kv_s, win_all[:, W + T - n_keep:]


def sq_relu_mlp(h, w_up, w_down):
    return jnp.square(jax.nn.relu(h @ w_up)) @ w_down


def setup_inputs(seed: int = 0) -> dict:
    key = jax.random.key(seed)
    ks = jax.random.split(key, 24)
    n_pages = PAST_LEN // PAGE_SIZE
    n_used = DEC_BATCH * n_pages
    n_phys = n_used + max(1, n_used // 4)
    win_len = min(WINDOW, PAST_LEN)

    def nrm(k, shape, scale=1.0):
        return jax.random.normal(k, shape, jnp.float32) * scale

    def gain(k, shape):
        return 1.0 + 0.02 * jax.random.normal(k, shape, jnp.float32)

    page_table = jax.random.permutation(ks[6], n_phys)[:n_used].reshape(DEC_BATCH, n_pages).astype(jnp.int32)
    return {
        "x_prompt": nrm(ks[0], (BATCH, SEQ, D_MODEL)),
        "x_sample": nrm(ks[1], (DEC_BATCH, DEC_SEQ, D_MODEL)),
        "cache_sb_kv": nrm(ks[2], (DEPTH, n_phys, PAGE_SIZE, 2, SB_HEADS, HEAD_DIM)),
        "cache_nsa_cmp_kv": nrm(ks[3], (DEPTH, n_phys, PAGE_SIZE, 2, NSA_KV_HEADS, HEAD_DIM)),
        "cache_nsa_slc_kv": nrm(ks[4], (DEPTH, n_phys, PAGE_SIZE, 2, NSA_KV_HEADS, HEAD_DIM)),
        "state_nsa_win_kv": nrm(ks[5], (DEPTH, DEC_BATCH, win_len, 2, NSA_KV_HEADS, HEAD_DIM)),
        "page_table": page_table,
        "w_in": nrm(ks[7], (DEPTH, D_MODEL, IN_COLS), D_MODEL ** -0.5),
        "w_cmp_k1": nrm(ks[8], (DEPTH, CMP_BLK * HEAD_DIM, CMP_HID), (CMP_BLK * HEAD_DIM) ** -0.5),
        "w_cmp_k2": nrm(ks[9], (DEPTH, CMP_HID, HEAD_DIM), CMP_HID ** -0.5),
        "w_cmp_v1": nrm(ks[10], (DEPTH, CMP_BLK * HEAD_DIM, CMP_HID), (CMP_BLK * HEAD_DIM) ** -0.5),
        "w_cmp_v2": nrm(ks[11], (DEPTH, CMP_HID, HEAD_DIM), CMP_HID ** -0.5),
        "pe_cmp_k": nrm(ks[12], (DEPTH, CMP_BLK, HEAD_DIM), 0.1),
        "pe_cmp_v": nrm(ks[13], (DEPTH, CMP_BLK, HEAD_DIM), 0.1),
        "g_sb_out": gain(ks[14], (DEPTH, SB_WIDTH)),
        "g_nsa_out": gain(ks[15], (DEPTH, NSA_WIDTH)),
        "w_out": nrm(ks[16], (DEPTH, MIX_WIDTH, D_MODEL), MIX_WIDTH ** -0.5),
        "g_attn_norm": gain(ks[17], (DEPTH, D_MODEL)),
        "g_mlp_norm": gain(ks[18], (DEPTH, D_MODEL)),
        "w_up": nrm(ks[19], (DEPTH, D_MODEL, D_FF), D_MODEL ** -0.5),
        "w_down": nrm(ks[20], (DEPTH, D_FF, D_MODEL), D_FF ** -0.5),
        "g_final": gain(ks[21], (D_MODEL,)),
    }


def reference(x_prompt, x_sample, cache_sb_kv, cache_nsa_cmp_kv, cache_nsa_slc_kv, state_nsa_win_kv, page_table,
              w_in, w_cmp_k1, w_cmp_k2, w_cmp_v1, w_cmp_v2, pe_cmp_k, pe_cmp_v, g_sb_out, g_nsa_out, w_out,
              g_attn_norm, g_mlp_norm, w_up, w_down, g_final):
    xp, xs = x_prompt, x_sample
    sb_p, sb_s, cmp_p, cmp_s, slc_p, slc_s, win_p, win_s = [], [], [], [], [], [], [], []
    for l in range(DEPTH):
        cw = (w_cmp_k1[l], w_cmp_k2[l], w_cmp_v1[l], w_cmp_v2[l], pe_cmp_k[l], pe_cmp_v[l])
        mix, a, b, c, d = mixer_prompt(rmsnorm(xp, g_attn_norm[l]), w_in[l], cw, g_sb_out[l], g_nsa_out[l], w_out[l])
        xp = xp + mix
        xp = xp + sq_relu_mlp(rmsnorm(xp, g_mlp_norm[l]), w_up[l], w_down[l])
        sb_p.append(a); cmp_p.append(b); slc_p.append(c); win_p.append(d)
        mix, a, b, c, d = mixer_sample(rmsnorm(xs, g_attn_norm[l]), cache_sb_kv[l], cache_nsa_cmp_kv[l],
                                       cache_nsa_slc_kv[l], state_nsa_win_kv[l], page_table,
                                       w_in[l], cw, g_sb_out[l], g_nsa_out[l], w_out[l])
        xs = xs + mix
        xs = xs + sq_relu_mlp(rmsnorm(xs, g_mlp_norm[l]), w_up[l], w_down[l])
        sb_s.append(a); cmp_s.append(b); slc_s.append(c); win_s.append(d)
    y_prompt = rmsnorm(xp, g_final)
    y_sample = rmsnorm(xs, g_final)
    return (y_prompt, y_sample,
            jnp.stack(sb_p), jnp.stack(sb_s),
            jnp.stack(cmp_p), jnp.stack(cmp_s),
            jnp.stack(slc_p), jnp.stack(slc_s),
            jnp.stack(win_p), jnp.stack(win_s))
```

```python
import functools
import math

import jax
import jax.numpy as jnp
import numpy as np
from jax import lax
from jax.experimental import pallas as pl
from jax.experimental.pallas import tpu as pltpu

F32 = jnp.float32
BF16 = jnp.bfloat16

HEAD_DIM = 64
SB_HEADS = 8
NSA_HEADS = 8
NSA_KV_HEADS = 2
NSA_GROUP = NSA_HEADS // NSA_KV_HEADS
SB_WIDTH = SB_HEADS * HEAD_DIM
NSA_WIDTH = NSA_HEADS * HEAD_DIM
KV_WIDTH = NSA_KV_HEADS * HEAD_DIM
KV2 = 2 * KV_WIDTH
N_GATES = 3 * NSA_HEADS
CMP_STRIDE = 16
CMP_RATIO = 2
CMP_BLK = CMP_STRIDE * CMP_RATIO
CMP_HID = 2 * HEAD_DIM
SEL_BLK = 64
TOP_N = 16
N_LOCAL = 2
WINDOW = 512
ROPE_THETA = 10000.0
EPS = 1e-6
NEG = -1e30
TINY = 1e-30
FORCE = 1e9
PAD_IMP = -3e38
LANES = 128
VMEM_LIMIT = 56 * 1024 * 1024
PAGES_PER_STEP = 8

_NT = (((1,), (1,)), ((), ()))


def _dot(a, b):
    return jnp.dot(a, b, preferred_element_type=F32)


def _dot_nt(a, b):
    return lax.dot_general(a, b, _NT, preferred_element_type=F32)


def _split_dot(x, m):
    hi = x.astype(BF16)
    lo = (x - hi.astype(F32)).astype(BF16)
    return _dot(hi, m) + _dot(lo, m)


def _rms(x, g):
    return x * lax.rsqrt(jnp.mean(x * x, axis=-1, keepdims=True) + EPS) * g


def _softplus(z):
    return jnp.maximum(z, 0.0) + jnp.log(1.0 + jnp.exp(-jnp.abs(z)))


def _iota(shape, axis):
    return lax.broadcasted_iota(jnp.int32, shape, axis)


def _params(*sem):
    return pltpu.CompilerParams(dimension_semantics=sem, vmem_limit_bytes=VMEM_LIMIT)


def _proj_kernel(x_ref, g_ref, w_ref, cos_ref, sin_ref,
                 qsb_ref, sbkv_ref, qn_ref, kvc_ref, kvs_ref, kvw_ref, gate_ref):
    h = _rms(x_ref[...], g_ref[...]).astype(BF16)
    cos = cos_ref[...]
    sin = sin_ref[...]

    def mm(lo, hi):
        return _dot(h, w_ref[:, lo:hi])

    def rope(z):
        w = z.shape[1]
        reps = w // LANES
        c = jnp.concatenate([cos] * reps, axis=1) if reps > 1 else cos
        s = jnp.concatenate([sin] * reps, axis=1) if reps > 1 else sin
        first = (_iota(z.shape, 1) % HEAD_DIM) < (HEAD_DIM // 2)
        rot = jnp.where(first, pltpu.roll(z, w - HEAD_DIM // 2, 1), pltpu.roll(z, HEAD_DIM // 2, 1))
        return z * c + rot * s

    o = 0
    qsb_ref[...] = mm(o, o + SB_WIDTH)
    o += SB_WIDTH
    sbkv_ref[...] = mm(o, o + 2 * SB_WIDTH)
    o += 2 * SB_WIDTH
    qn_ref[...] = rope(mm(o, o + NSA_WIDTH))
    o += NSA_WIDTH
    for ref in (kvc_ref, kvs_ref, kvw_ref):
        z = mm(o, o + KV2)
        ref[:, :KV_WIDTH] = rope(z[:, :KV_WIDTH])
        ref[:, KV_WIDTH:] = z[:, KV_WIDTH:]
        o += KV2
    g = mm(o, o + N_GATES)
    gate_ref[...] = 1.0 / (1.0 + jnp.exp(-g))


def _rope_tables(pos):
    half = HEAD_DIM // 2
    inv = ROPE_THETA ** (-jnp.arange(half, dtype=F32) / half)
    ang = pos.astype(F32)[:, None] * inv[None, :]
    cos, sin = jnp.cos(ang), jnp.sin(ang)
    reps = LANES // HEAD_DIM
    return (jnp.concatenate([cos, cos] * reps, axis=1), jnp.concatenate([-sin, sin] * reps, axis=1))


def _project(x2d, g, w_bf16, cos, sin, tm):
    n, d = x2d.shape
    n_tab = cos.shape[0] // tm
    row = lambda i: (i, 0)
    tab = lambda i: (i % n_tab, 0)
    fixed = lambda i: (0, 0)
    widths = (SB_WIDTH, 2 * SB_WIDTH, NSA_WIDTH, KV2, KV2, KV2, N_GATES)
    return pl.pallas_call(
        _proj_kernel,
        grid=(n // tm,),
        in_specs=[pl.BlockSpec((tm, d), row), pl.BlockSpec((1, d), fixed),
                  pl.BlockSpec(w_bf16.shape, fixed),
                  pl.BlockSpec((tm, LANES), tab), pl.BlockSpec((tm, LANES), tab)],
        out_specs=[pl.BlockSpec((tm, w), row) for w in widths],
        out_shape=[jax.ShapeDtypeStruct((n, w), F32) for w in widths],
        compiler_params=_params("parallel"),
        name="proj",
    )(x2d, g.reshape(1, d), w_bf16, cos, sin)


def _sbp_kernel(q_ref, k_ref, v_ref, o_ref, acc_ref, run_ref, *, blk):
    i = pl.program_id(2)
    lane = _iota((blk, LANES), 1)
    q = q_ref[0] * (HEAD_DIM ** -0.5)
    qh = [jnp.where(lane < HEAD_DIM, q, 0.0).astype(BF16), jnp.where(lane >= HEAD_DIM, q, 0.0).astype(BF16)]
    later = (_iota((blk, blk), 0) > _iota((blk, blk), 1)).astype(BF16)
    acc_ref[...] = jnp.zeros_like(acc_ref)
    run_ref[...] = jnp.zeros_like(run_ref)

    def step(j, mask):
        kb = k_ref[0, pl.ds(j * blk, blk), :].astype(BF16)
        vb = v_ref[0, pl.ds(j * blk, blk), :].astype(BF16)
        for h in range(2):
            z = _dot_nt(qh[h], kb)
            sp = _softplus(z)
            log_1m = -sp if mask is None else jnp.where(mask, -sp, 0.0)
            after = _split_dot(log_1m, later) + run_ref[h]
            a = jnp.exp((z - sp) + after)
            if mask is not None:
                a = jnp.where(mask, a, 0.0)
            acc_ref[h] += _dot(a.astype(BF16), vb)
            run_ref[h] += jnp.sum(log_1m, axis=1, keepdims=True)

    step(i, _iota((blk, blk), 1) < _iota((blk, blk), 0))

    def body(jj, c):
        step(i - 1 - jj, None)
        return c

    lax.fori_loop(0, i, body, 0)
    o_ref[0] = jnp.where(lane < HEAD_DIM, acc_ref[0], acc_ref[1])


def _sb_prompt(q_sb, sb_kv, blk):
    b, t, _ = q_sb.shape
    pairs = SB_WIDTH // LANES
    return pl.pallas_call(
        functools.partial(_sbp_kernel, blk=blk),
        grid=(b, pairs, t // blk),
        in_specs=[pl.BlockSpec((1, blk, LANES), lambda bi, p, i: (bi, i, p)),
                  pl.BlockSpec((1, t, LANES), lambda bi, p, i: (bi, 0, p)),
                  pl.BlockSpec((1, t, LANES), lambda bi, p, i: (bi, 0, pairs + p))],
        out_specs=pl.BlockSpec((1, blk, LANES), lambda bi, p, i: (bi, i, p)),
        out_shape=jax.ShapeDtypeStruct((b, t, SB_WIDTH), F32),
        scratch_shapes=[pltpu.VMEM((2, blk, LANES), F32), pltpu.VMEM((2, blk, 1), F32)],
        compiler_params=_params("parallel", "parallel", "arbitrary"),
        name="sb_prompt",
    )(q_sb, sb_kv, sb_kv)


def _compress_weights(w_k1, w_k2, w_v1, w_v2, pe_k, pe_v):
    nkv = 2 * NSA_KV_HEADS
    eye = jnp.eye(nkv, dtype=F32)
    w1 = jnp.stack([w_k1] * NSA_KV_HEADS + [w_v1] * NSA_KV_HEADS)
    w1 = w1.reshape(nkv, CMP_RATIO, CMP_STRIDE, HEAD_DIM, CMP_HID)
    wab = jnp.einsum('shidn,st->itdhsn', w1, eye).reshape(CMP_STRIDE * nkv * HEAD_DIM, CMP_RATIO * nkv * CMP_HID)
    w2 = jnp.stack([w_k2] * NSA_KV_HEADS + [w_v2] * NSA_KV_HEADS)
    w2b = jnp.einsum('snd,st->sntd', w2, eye).reshape(nkv * CMP_HID, nkv * HEAD_DIM)
    pe = jnp.stack([pe_k] * NSA_KV_HEADS + [pe_v] * NSA_KV_HEADS)
    pe = pe.reshape(nkv, CMP_RATIO, CMP_STRIDE, HEAD_DIM).transpose(1, 2, 0, 3).reshape(CMP_RATIO, -1)
    return wab.astype(BF16), w2b.astype(BF16), pe


def _compress_tail(x, wab_ref, w2_ref, pe_ref, out_ref):
    n = x.shape[0]
    half = wab_ref.shape[1] // 2
    xa = (x + pe_ref[0:1, :]).astype(BF16)
    xb = (x + pe_ref[1:2, :]).astype(BF16)
    hid = _dot(xa, wab_ref[:, :half]) + pltpu.roll(_dot(xb, wab_ref[:, half:]), n - 1, 0)
    act = hid / (1.0 + jnp.exp(-hid))
    out = _dot(act.astype(BF16), w2_ref[...])
    for s in range(2 * NSA_KV_HEADS):
        out_ref[0, s] = out[:, s * HEAD_DIM:(s + 1) * HEAD_DIM]


def _compress_prompt_kernel(x_ref, wab_ref, w2_ref, pe_ref, out_ref):
    _compress_tail(x_ref[0], wab_ref, w2_ref, pe_ref, out_ref)


def _compress_prompt(kv_c, cw):
    wab, w2b, pe = cw
    b, t, _ = kv_c.shape
    n_chunks = t // CMP_STRIDE
    feat = CMP_STRIDE * KV2
    x = kv_c[:, :n_chunks * CMP_STRIDE].reshape(b, n_chunks, feat)
    fixed = lambda bi: (0, 0)
    return pl.pallas_call(
        _compress_prompt_kernel,
        grid=(b,),
        in_specs=[pl.BlockSpec((1, n_chunks, feat), lambda bi: (bi, 0, 0)),
                  pl.BlockSpec(wab.shape, fixed), pl.BlockSpec(w2b.shape, fixed), pl.BlockSpec(pe.shape, fixed)],
        out_specs=pl.BlockSpec((1, 2 * NSA_KV_HEADS, n_chunks, HEAD_DIM), lambda bi: (bi, 0, 0, 0)),
        out_shape=jax.ShapeDtypeStruct((b, 2 * NSA_KV_HEADS, n_chunks, HEAD_DIM), F32),
        compiler_params=_params("parallel"),
        name="compress_prompt",
    )(x, wab, w2b, pe)


def _compress_decode_kernel(pt_ref, *refs, rows):
    del pt_ref
    g = PAGES_PER_STEP
    pages, (wab_ref, w2_ref, pe_ref, out_ref, x_ref) = refs[:g], refs[g:]
    pg = pl.program_id(1)
    for k in range(g):
        x_ref[pl.ds((pg * g + k) * rows, rows), :] = pages[k][0]

    @pl.when(pg == pl.num_programs(1) - 1)
    def _():
        _compress_tail(x_ref[...], wab_ref, w2_ref, pe_ref, out_ref)


def _compress_decode(cache, page_table, cw):
    wab, w2b, pe = cw
    b, n_pages = page_table.shape
    n_phys, page = cache.shape[:2]
    rows = page // CMP_STRIDE
    feat = CMP_STRIDE * KV2
    x = cache.reshape(n_phys, rows, feat)
    g = PAGES_PER_STEP
    n_chunks = n_pages * rows
    fixed = lambda bi, pg, pt: (0, 0)
    page_specs = [pl.BlockSpec((1, rows, feat), functools.partial(lambda bi, pg, pt, k: (pt[bi, pg * g + k], 0, 0), k=k))
                  for k in range(g)]
    grid_spec = pltpu.PrefetchScalarGridSpec(
        num_scalar_prefetch=1,
        grid=(b, n_pages // g),
        in_specs=page_specs + [pl.BlockSpec(wab.shape, fixed), pl.BlockSpec(w2b.shape, fixed),
                               pl.BlockSpec(pe.shape, fixed)],
        out_specs=pl.BlockSpec((1, 2 * NSA_KV_HEADS, n_chunks, HEAD_DIM), lambda bi, pg, pt: (bi, 0, 0, 0)),
        scratch_shapes=[pltpu.VMEM((n_chunks, feat), F32)],
    )
    return pl.pallas_call(
        functools.partial(_compress_decode_kernel, rows=rows),
        grid_spec=grid_spec,
        out_shape=jax.ShapeDtypeStruct((b, 2 * NSA_KV_HEADS, n_chunks, HEAD_DIM), F32),
        compiler_params=_params("parallel", "arbitrary"),
        name="compress_decode",
    )(page_table, *([x] * g), wab, w2b, pe)


def _place(x, lane_off, width):
    parts = []
    if lane_off:
        parts.append(jnp.zeros((x.shape[0], lane_off), x.dtype))
    parts.append(x)
    rest = width - lane_off - x.shape[1]
    if rest:
        parts.append(jnp.zeros((x.shape[0], rest), x.dtype))
    return jnp.concatenate(parts, axis=1) if len(parts) > 1 else x


def _nsa_prompt_kernel(q_ref, cmp_ref, kvs_ref, kvw_ref, gate_ref, o_ref, m_ref, l_ref, acc_ref, sel_ref,
                       *, blk, n_cmp, n_sel):
    i = pl.program_id(1)
    t_len = kvs_ref.shape[1]
    n_c = cmp_ref.shape[2]
    rows = NSA_GROUP * blk
    q_all = q_ref[0] * (HEAD_DIM ** -0.5)
    lane_kv = _iota((blk, KV2), 1)
    q_pos = i * blk + _iota((blk, 1), 0)
    gates = gate_ref[0]

    c_idx = _iota((blk, n_c), 1)
    cmp_ok = (c_idx * CMP_STRIDE + (CMP_BLK - 1) <= q_pos) & (c_idx < n_cmp)
    oc = _iota((n_c, LANES), 0) * CMP_STRIDE
    oj = _iota((n_c, LANES), 1) * SEL_BLK
    overlap = ((oc < oj + SEL_BLK) & (oc + CMP_BLK > oj)).astype(BF16)
    j_idx = _iota((blk, LANES), 1)
    cur = q_pos // SEL_BLK
    forced = (j_idx == 0) | ((j_idx <= cur) & (j_idx > cur - N_LOCAL))
    expand = (_iota((LANES, t_len), 0) == _iota((LANES, t_len), 1) // SEL_BLK).astype(BF16)

    def attend(qp, kv_ref, first_tile, n_tiles, mask_fn):
        m_ref[...] = jnp.full_like(m_ref, NEG)
        l_ref[...] = jnp.zeros_like(l_ref)
        acc_ref[...] = jnp.zeros_like(acc_ref)

        def body(jj, c):
            jt = first_tile + jj
            kv = kv_ref[0, pl.ds(jt * blk, blk), :].astype(BF16)
            ok = mask_fn(jt)
            okf = jnp.concatenate([ok] * NSA_GROUP, axis=0)
            s = jnp.where(okf, _dot_nt(qp, kv), NEG)
            m_new = jnp.maximum(m_ref[...], jnp.max(s, axis=1, keepdims=True))
            p = jnp.where(okf, jnp.exp(s - m_new), 0.0)
            alpha = jnp.exp(m_ref[...] - m_new)
            l_ref[...] = alpha * l_ref[...] + jnp.sum(p, axis=1, keepdims=True)
            acc_ref[...] = alpha * acc_ref[...] + _dot(p.astype(BF16), kv)
            m_ref[...] = m_new
            return c

        lax.fori_loop(0, n_tiles, body, 0)
        return acc_ref[...] / jnp.maximum(l_ref[...], TINY)

    o_cs, qps = [], []
    for kh in range(NSA_KV_HEADS):
        qk = q_all[:, kh * NSA_GROUP * HEAD_DIM:(kh + 1) * NSA_GROUP * HEAD_DIM]
        qk_bf = qk.astype(BF16)

        kc = cmp_ref[0, kh]
        vc = cmp_ref[0, NSA_KV_HEADS + kh]
        probs = []
        for g in range(NSA_GROUP):
            s = _dot_nt(qk_bf, _place(kc, g * HEAD_DIM, KV2).astype(BF16))
            s = jnp.where(cmp_ok, s, NEG)
            p = jnp.where(cmp_ok, jnp.exp(s - jnp.max(s, axis=1, keepdims=True)), 0.0)
            probs.append(p / jnp.maximum(jnp.sum(p, axis=1, keepdims=True), TINY))
        vc_rows = jnp.concatenate([_place(vc, g * HEAD_DIM, KV2) for g in range(NSA_GROUP)], axis=0)
        o_c = _dot(jnp.concatenate(probs, axis=1).astype(BF16), vc_rows.astype(BF16))
        p_sum = (probs[0] + probs[1]) + (probs[2] + probs[3])

        imp = _split_dot(p_sum, overlap)
        imp = jnp.where(forced, FORCE, imp)
        imp = jnp.where(j_idx * SEL_BLK <= q_pos, imp, NEG)
        imp = jnp.where(j_idx < n_sel, imp, PAD_IMP)
        rank = jnp.zeros((blk, LANES), F32)
        for k in range(n_sel):
            col = imp[:, k:k + 1]
            better = (col > imp) | ((col == imp) & (j_idx > k))
            rank = rank + jnp.where(better, 1.0, 0.0)
        sel = jnp.where(rank < float(min(TOP_N, n_sel)), 1.0, 0.0).astype(BF16)
        sel_ref[kh] = _dot(sel, expand)

        shift = [pltpu.roll(qk, (kh - g) * HEAD_DIM % KV2, 1) if (kh - g) % NSA_GROUP else qk for g in range(NSA_GROUP)]
        on_k = (lane_kv >= kh * HEAD_DIM) & (lane_kv < (kh + 1) * HEAD_DIM)
        qps.append(jnp.concatenate([jnp.where(on_k, x, 0.0) for x in shift], axis=0).astype(BF16))
        o_cs.append(o_c)

    out_parts = []
    for kh in range(NSA_KV_HEADS):
        o_c, qp = o_cs[kh], qps[kh]

        def sel_mask(jt, kh=kh):
            key = jt * blk + _iota((blk, blk), 1)
            return (sel_ref[kh, :, pl.ds(pl.multiple_of(jt * blk, blk), blk)] > 0.5) & (key <= q_pos)

        def win_mask(jt):
            key = jt * blk + _iota((blk, blk), 1)
            return (key <= q_pos) & (key > q_pos - WINDOW)

        o_s = attend(qp, kvs_ref, 0, i + 1, sel_mask)
        w_tiles = -(-WINDOW // blk)
        first_w = jnp.maximum(i - w_tiles, 0)
        o_w = attend(qp, kvw_ref, first_w, i + 1 - first_w, win_mask)

        v_lo = KV_WIDTH + kh * HEAD_DIM
        res = jnp.zeros((blk, KV2), F32)
        for g in range(NSA_GROUP):
            head = kh * NSA_GROUP + g
            on_g = (lane_kv >= g * HEAD_DIM) & (lane_kv < (g + 1) * HEAD_DIM)
            sh = (g * HEAD_DIM - v_lo) % KV2
            osg = o_s[g * blk:(g + 1) * blk]
            owg = o_w[g * blk:(g + 1) * blk]
            mix = gates[:, 3 * head + 1:3 * head + 2] * osg + gates[:, 3 * head + 2:3 * head + 3] * owg
            mix = pltpu.roll(mix, sh, 1) if sh else mix
            res = res + jnp.where(on_g, mix + gates[:, 3 * head:3 * head + 1] * o_c, 0.0)
        out_parts.append(res)
    o_ref[0] = jnp.concatenate(out_parts, axis=1)


def _nsa_prompt(q_n, cmp_kv, kv_s, kv_w, gates, blk):
    b, t, _ = q_n.shape
    n_c = cmp_kv.shape[2]
    n_cmp = t // CMP_STRIDE - CMP_RATIO + 1
    n_sel = -(-t // SEL_BLK)
    rows = NSA_GROUP * blk
    return pl.pallas_call(
        functools.partial(_nsa_prompt_kernel, blk=blk, n_cmp=n_cmp, n_sel=n_sel),
        grid=(b, t // blk),
        in_specs=[pl.BlockSpec((1, blk, NSA_WIDTH), lambda bi, i: (bi, i, 0)),
                  pl.BlockSpec((1, 2 * NSA_KV_HEADS, n_c, HEAD_DIM), lambda bi, i: (bi, 0, 0, 0)),
                  pl.BlockSpec((1, t, KV2), lambda bi, i: (bi, 0, 0)),
                  pl.BlockSpec((1, t, KV2), lambda bi, i: (bi, 0, 0)),
                  pl.BlockSpec((1, blk, N_GATES), lambda bi, i: (bi, i, 0))],
        out_specs=pl.BlockSpec((1, blk, NSA_WIDTH), lambda bi, i: (bi, i, 0)),
        out_shape=jax.ShapeDtypeStruct((b, t, NSA_WIDTH), F32),
        scratch_shapes=[pltpu.VMEM((rows, 1), F32), pltpu.VMEM((rows, 1), F32), pltpu.VMEM((rows, KV2), F32),
                        pltpu.VMEM((NSA_KV_HEADS, blk, t), F32)],
        compiler_params=_params("parallel", "arbitrary"),
        name="nsa_prompt",
    )(q_n, cmp_kv, kv_s, kv_w, gates)


def _out_mlp_kernel(x_ref, osb_ref, on_ref, gsb_ref, gn_ref, wo_ref, gm_ref, wu_ref, wd_ref, gf_ref, y_ref):
    a = _rms(osb_ref[...], gsb_ref[...]).astype(BF16)
    b = _rms(on_ref[...], gn_ref[...]).astype(BF16)
    x1 = x_ref[...] + _dot(a, wo_ref[:SB_WIDTH, :]) + _dot(b, wo_ref[SB_WIDTH:, :])
    h = _rms(x1, gm_ref[...]).astype(BF16)
    u = jnp.maximum(_dot(h, wu_ref[...]), 0.0)
    x2 = x1 + _dot((u * u).astype(BF16), wd_ref[...])
    y_ref[...] = _rms(x2, gf_ref[...])


def _out_mlp(x2d, o_sb, o_n, g_sb, g_nsa, w_out, g_mlp, w_up, w_down, g_final, tm):
    n, d = x2d.shape
    row = lambda i: (i, 0)
    fixed = lambda i: (0, 0)
    once = functools.partial(pl.BlockSpec, index_map=fixed)
    return pl.pallas_call(
        _out_mlp_kernel,
        grid=(n // tm,),
        in_specs=[pl.BlockSpec((tm, d), row), pl.BlockSpec((tm, SB_WIDTH), row), pl.BlockSpec((tm, NSA_WIDTH), row),
                  once((1, SB_WIDTH)), once((1, NSA_WIDTH)), once(w_out.shape), once((1, d)),
                  once(w_up.shape), once(w_down.shape), once((1, d))],
        out_specs=pl.BlockSpec((tm, d), row),
        out_shape=jax.ShapeDtypeStruct((n, d), F32),
        compiler_params=_params("parallel"),
        name="out_mlp",
    )(x2d, o_sb, o_n, g_sb.reshape(1, -1), g_nsa.reshape(1, -1), w_out, g_mlp.reshape(1, -1), w_up, w_down,
      g_final.reshape(1, -1))


def _heads_on_lanes(x, width, group):
    reps = width // HEAD_DIM
    tiled = jnp.concatenate([x] * reps, axis=1)
    slot = _iota(tiled.shape, 1) // HEAD_DIM
    row = _iota(tiled.shape, 0)
    return jnp.where(slot == row // group, tiled, 0.0)


def _sb_decode_kernel(pt_ref, q_ref, *refs, page):
    del pt_ref
    g = PAGES_PER_STEP
    pages, (o_ref, acc_ref, run_ref) = refs[:g], refs[g:]
    pg = pl.program_id(1)

    @pl.when(pg == 0)
    def _():
        acc_ref[...] = jnp.zeros_like(acc_ref)
        run_ref[...] = jnp.zeros_like(run_ref)

    q = _heads_on_lanes(q_ref[0] * (HEAD_DIM ** -0.5), SB_WIDTH, 1).astype(BF16)
    later = (_iota((page, page), 0) > _iota((page, page), 1)).astype(BF16)
    kcat = jnp.concatenate([p[0, :, :SB_WIDTH].astype(BF16) for p in pages], axis=0)
    vcat = jnp.concatenate([p[0, :, SB_WIDTH:].astype(BF16) for p in pages], axis=0)
    z = _dot_nt(q, kcat)
    sp = _softplus(z)
    log_1m = -sp
    chunks = [log_1m[:, c * page:(c + 1) * page] for c in range(g)]
    within = _split_dot(jnp.concatenate(chunks, axis=0), later)
    run = run_ref[...]
    after = [None] * g
    for c in reversed(range(g)):
        after[c] = within[c * SB_HEADS:(c + 1) * SB_HEADS] + run
        run = run + jnp.sum(chunks[c], axis=1, keepdims=True)
    run_ref[...] = run
    a = jnp.exp((z - sp) + jnp.concatenate(after, axis=1))
    acc_ref[...] += _dot(a.astype(BF16), vcat)

    @pl.when(pg == pl.num_programs(1) - 1)
    def _():
        acc = acc_ref[...]
        row = _iota((SB_HEADS, HEAD_DIM), 0)
        out = jnp.zeros((SB_HEADS, HEAD_DIM), F32)
        for h in range(SB_HEADS):
            out = out + jnp.where(row == h, acc[:, h * HEAD_DIM:(h + 1) * HEAD_DIM], 0.0)
        o_ref[0] = out


def _sb_decode(q_sb, cache, page_table):
    b, n_pages = page_table.shape
    n_phys, page = cache.shape[:2]
    x = cache.reshape(n_phys, page, 2 * SB_WIDTH)
    g = PAGES_PER_STEP
    n_steps = n_pages // g

    def page_map(bi, pg, pt, k):
        return (pt[bi, (n_steps - 1 - pg) * g + k], 0, 0)

    grid_spec = pltpu.PrefetchScalarGridSpec(
        num_scalar_prefetch=1,
        grid=(b, n_steps),
        in_specs=[pl.BlockSpec((1, SB_HEADS, HEAD_DIM), lambda bi, pg, pt: (bi, 0, 0))]
        + [pl.BlockSpec((1, page, 2 * SB_WIDTH), functools.partial(page_map, k=k)) for k in range(g)],
        out_specs=pl.BlockSpec((1, SB_HEADS, HEAD_DIM), lambda bi, pg, pt: (bi, 0, 0)),
        scratch_shapes=[pltpu.VMEM((SB_HEADS, SB_WIDTH), F32), pltpu.VMEM((SB_HEADS, 1), F32)],
    )
    return pl.pallas_call(
        functools.partial(_sb_decode_kernel, page=page),
        grid_spec=grid_spec,
        out_shape=jax.ShapeDtypeStruct((b, SB_HEADS, HEAD_DIM), F32),
        compiler_params=_params("parallel", "arbitrary"),
        name="sb_decode",
    )(page_table, q_sb.reshape(b, SB_HEADS, HEAD_DIM), *([x] * g))


def _cmp_decode_kernel(q_ref, cmp_ref, oc_ref, sel_ref, *, n_cmp, n_sel, q_pos, sel_lanes):
    n_c = cmp_ref.shape[2]
    q = q_ref[0] * (HEAD_DIM ** -0.5)
    row = _iota((NSA_HEADS, 1), 0)
    c_idx = _iota((NSA_HEADS, n_c), 1)
    ok = (c_idx * CMP_STRIDE + (CMP_BLK - 1) <= q_pos) & (c_idx < n_cmp)
    s = jnp.zeros((NSA_HEADS, n_c), F32)
    for kh in range(NSA_KV_HEADS):
        s = s + jnp.where(row // NSA_GROUP == kh, _dot_nt(q.astype(BF16), cmp_ref[0, kh].astype(BF16)), 0.0)
    s = jnp.where(ok, s, NEG)
    p = jnp.where(ok, jnp.exp(s - jnp.max(s, axis=1, keepdims=True)), 0.0)
    p = p / jnp.maximum(jnp.sum(p, axis=1, keepdims=True), TINY)
    o_c = jnp.zeros((NSA_HEADS, HEAD_DIM), F32)
    for kh in range(NSA_KV_HEADS):
        o_c = o_c + jnp.where(row // NSA_GROUP == kh,
                              _dot(p.astype(BF16), cmp_ref[0, NSA_KV_HEADS + kh].astype(BF16)), 0.0)
    oc_ref[0] = o_c

    grp = jnp.zeros((NSA_HEADS, n_c), F32)
    for kh in range(NSA_KV_HEADS):
        tot = jnp.sum(jnp.where(row // NSA_GROUP == kh, p, 0.0), axis=0, keepdims=True)
        grp = grp + jnp.where(row // NSA_GROUP == kh, tot, 0.0)
    oc_ = _iota((n_c, sel_lanes), 0) * CMP_STRIDE
    oj = _iota((n_c, sel_lanes), 1) * SEL_BLK
    overlap = ((oc_ < oj + SEL_BLK) & (oc_ + CMP_BLK > oj)).astype(BF16)
    imp = _split_dot(grp, overlap)
    j_idx = _iota((NSA_HEADS, sel_lanes), 1)
    cur = q_pos // SEL_BLK
    forced = (j_idx == 0) | ((j_idx <= cur) & (j_idx > cur - N_LOCAL))
    imp = jnp.where(forced, FORCE, imp)
    imp = jnp.where(j_idx * SEL_BLK <= q_pos, imp, NEG)
    imp = jnp.where(j_idx < n_sel, imp, PAD_IMP)
    kk = _iota((sel_lanes, sel_lanes), 0)
    jj = _iota((sel_lanes, sel_lanes), 1)
    eye = kk == jj
    for kh in range(NSA_KV_HEADS):
        r0 = kh * NSA_GROUP
        imp_row = imp[r0:r0 + 1, :]
        imp_col = jnp.sum(jnp.where(eye, imp_row, 0.0), axis=1, keepdims=True)
        better = (imp_col > imp_row) | ((imp_col == imp_row) & (kk < jj))
        rank = jnp.sum(jnp.where(better, 1.0, 0.0), axis=0, keepdims=True)
        sel = jnp.where(rank < float(min(TOP_N, n_sel)), 1.0, 0.0)
        sel_ref[0, r0:r0 + NSA_GROUP, :] = jnp.broadcast_to(sel, (NSA_GROUP, sel_lanes))


def _cmp_decode(q_n, cmp_kv, q_pos, n_cmp, n_sel):
    b = q_n.shape[0]
    n_c = cmp_kv.shape[2]
    sel_lanes = -(-n_sel // LANES) * LANES
    return pl.pallas_call(
        functools.partial(_cmp_decode_kernel, n_cmp=n_cmp, n_sel=n_sel, q_pos=q_pos, sel_lanes=sel_lanes),
        grid=(b,),
        in_specs=[pl.BlockSpec((1, NSA_HEADS, HEAD_DIM), lambda bi: (bi, 0, 0)),
                  pl.BlockSpec((1, 2 * NSA_KV_HEADS, n_c, HEAD_DIM), lambda bi: (bi, 0, 0, 0))],
        out_specs=[pl.BlockSpec((1, NSA_HEADS, HEAD_DIM), lambda bi: (bi, 0, 0)),
                   pl.BlockSpec((1, NSA_HEADS, sel_lanes), lambda bi: (bi, 0, 0))],
        out_shape=[jax.ShapeDtypeStruct((b, NSA_HEADS, HEAD_DIM), F32),
                   jax.ShapeDtypeStruct((b, NSA_HEADS, sel_lanes), F32)],
        compiler_params=_params("parallel"),
        name="cmp_decode",
    )(q_n.reshape(b, NSA_HEADS, HEAD_DIM), cmp_kv)


def _sel_decode_kernel(pt_ref, q_ref, sel_ref, oc_ref, gate_ref, snew_ref, win_ref, wnew_ref, *refs,
                       page, past, win_len):
    del pt_ref
    g = PAGES_PER_STEP
    pages, (o_ref, m_ref, l_ref, acc_ref) = refs[:g], refs[g:]
    pg = pl.program_id(1)
    keys = g * page

    @pl.when(pg == 0)
    def _():
        m_ref[...] = jnp.full_like(m_ref, NEG)
        l_ref[...] = jnp.zeros_like(l_ref)
        acc_ref[...] = jnp.zeros_like(acc_ref)

    q = _heads_on_lanes(q_ref[0] * (HEAD_DIM ** -0.5), KV2, NSA_GROUP)
    q_bf = q.astype(BF16)
    sel = sel_ref[0]
    sel_lanes = sel.shape[1]

    def fold(s, ok, weigh):
        s = jnp.where(ok, s, NEG)
        m_new = jnp.maximum(m_ref[...], jnp.max(s, axis=1, keepdims=True))
        p = jnp.where(ok, jnp.exp(s - m_new), 0.0)
        alpha = jnp.exp(m_ref[...] - m_new)
        l_ref[...] = alpha * l_ref[...] + jnp.sum(p, axis=1, keepdims=True)
        acc_ref[...] = alpha * acc_ref[...] + weigh(p)
        m_ref[...] = m_new

    def fold_rows(kv, ok):
        kv = kv.astype(BF16)
        fold(_dot_nt(q_bf, kv), ok, lambda p: _dot(p.astype(BF16), kv))

    def fold_one(row, ok):
        row = jnp.broadcast_to(row, (NSA_HEADS, KV2))
        fold(jnp.sum(q * row, axis=1, keepdims=True), ok, lambda p: p * row)

    expand = (_iota((sel_lanes, keys), 0) == (pg * keys + _iota((sel_lanes, keys), 1)) // SEL_BLK).astype(BF16)
    fold_rows(jnp.concatenate([p[0] for p in pages], axis=0), _dot(sel.astype(BF16), expand) > 0.5)

    @pl.when(pg == pl.num_programs(1) - 1)
    def _():
        row = _iota((NSA_HEADS, HEAD_DIM), 0)

        def finish():
            o = acc_ref[...] / jnp.maximum(l_ref[...], TINY)
            out = jnp.zeros((NSA_HEADS, HEAD_DIM), F32)
            for kh in range(NSA_KV_HEADS):
                lo = KV_WIDTH + kh * HEAD_DIM
                out = out + jnp.where(row // NSA_GROUP == kh, o[:, lo:lo + HEAD_DIM], 0.0)
            return out

        new_blk = past // SEL_BLK
        fold_one(snew_ref[0], sel[:, new_blk:new_blk + 1] > 0.5)
        o_s = finish()

        m_ref[...] = jnp.full_like(m_ref, NEG)
        l_ref[...] = jnp.zeros_like(l_ref)
        acc_ref[...] = jnp.zeros_like(acc_ref)
        w_pos = past - win_len + _iota((NSA_HEADS, win_len), 1)
        fold_rows(win_ref[0], (w_pos <= past) & (w_pos > past - WINDOW) & (w_pos >= 0))
        fold_one(wnew_ref[0], jnp.full((NSA_HEADS, 1), True))
        o_w = finish()

        gates = gate_ref[0]
        o_ref[0] = gates[:, 0:1] * oc_ref[0] + gates[:, 1:2] * o_s + gates[:, 2:3] * o_w


def _sel_decode(q_n, sel, o_c, gates, kvs_new, win_buf, kvw_new, cache, page_table):
    b, n_pages = page_table.shape
    n_phys, page = cache.shape[:2]
    x = cache.reshape(n_phys, page, KV2)
    g = PAGES_PER_STEP
    past = n_pages * page
    win_len = win_buf.shape[1]
    sel_lanes = sel.shape[2]
    per_b = lambda bi, pg, pt: (bi, 0, 0)
    grid_spec = pltpu.PrefetchScalarGridSpec(
        num_scalar_prefetch=1,
        grid=(b, n_pages // g),
        in_specs=[pl.BlockSpec((1, NSA_HEADS, HEAD_DIM), per_b), pl.BlockSpec((1, NSA_HEADS, sel_lanes), per_b),
                  pl.BlockSpec((1, NSA_HEADS, HEAD_DIM), per_b), pl.BlockSpec((1, NSA_HEADS, 3), per_b),
                  pl.BlockSpec((1, 1, KV2), per_b), pl.BlockSpec((1, win_len, KV2), per_b),
                  pl.BlockSpec((1, 1, KV2), per_b)]
        + [pl.BlockSpec((1, page, KV2), functools.partial(lambda bi, pg, pt, k: (pt[bi, pg * g + k], 0, 0), k=k))
           for k in range(g)],
        out_specs=pl.BlockSpec((1, NSA_HEADS, HEAD_DIM), per_b),
        scratch_shapes=[pltpu.VMEM((NSA_HEADS, 1), F32), pltpu.VMEM((NSA_HEADS, 1), F32),
                        pltpu.VMEM((NSA_HEADS, KV2), F32)],
    )
    return pl.pallas_call(
        functools.partial(_sel_decode_kernel, page=page, past=past, win_len=win_len),
        grid_spec=grid_spec,
        out_shape=jax.ShapeDtypeStruct((b, NSA_HEADS, HEAD_DIM), F32),
        compiler_params=_params("parallel", "arbitrary"),
        name="sel_decode",
    )(page_table, q_n.reshape(b, NSA_HEADS, HEAD_DIM), sel, o_c, gates.reshape(b, NSA_HEADS, 3),
      kvs_new.reshape(b, 1, KV2), win_buf.reshape(b, win_len, KV2), kvw_new.reshape(b, 1, KV2), *([x] * g))


def kernel(x_prompt, x_sample, cache_sb_kv, cache_nsa_cmp_kv, cache_nsa_slc_kv, state_nsa_win_kv, page_table, w_in, w_cmp_k1, w_cmp_k2, w_cmp_v1, w_cmp_v2, pe_cmp_k, pe_cmp_v, g_sb_out, g_nsa_out, w_out, g_attn_norm, g_mlp_norm, w_up, w_down, g_final):
    depth = w_in.shape[0]
    assert depth == 1, "single-layer step"
    b, t, d = x_prompt.shape
    db, dt, _ = x_sample.shape
    assert dt == 1, "decode kernels handle one new token per sequence"
    n_pages = page_table.shape[1]
    page = cache_sb_kv.shape[2]
    past = n_pages * page
    l = 0

    w_in_b = w_in[l].astype(BF16)
    w_out_b = w_out[l].astype(BF16)
    w_up_b = w_up[l].astype(BF16)
    w_down_b = w_down[l].astype(BF16)
    cw = _compress_weights(w_cmp_k1[l], w_cmp_k2[l], w_cmp_v1[l], w_cmp_v2[l], pe_cmp_k[l], pe_cmp_v[l])

    blk = 256
    cos_p, sin_p = _rope_tables(jnp.arange(t))
    xp = x_prompt.reshape(b * t, d)
    q_sb, sb_kv, q_n, kv_c, kv_s, kv_w, gates = _project(xp, g_attn_norm[l], w_in_b, cos_p, sin_p, tm=512)
    o_sb = _sb_prompt(q_sb.reshape(b, t, -1), sb_kv.reshape(b, t, -1), blk)
    cmp_p = _compress_prompt(kv_c.reshape(b, t, KV2), cw)
    o_n = _nsa_prompt(q_n.reshape(b, t, -1), cmp_p, kv_s.reshape(b, t, KV2), kv_w.reshape(b, t, KV2),
                      gates.reshape(b, t, N_GATES), blk)
    y_prompt = _out_mlp(xp, o_sb.reshape(b * t, -1), o_n.reshape(b * t, -1), g_sb_out[l], g_nsa_out[l], w_out_b,
                        g_mlp_norm[l], w_up_b, w_down_b, g_final, tm=512).reshape(b, t, d)

    cos_s, sin_s = _rope_tables(jnp.full((db,), past))
    xs = x_sample.reshape(db, d)
    q_sb_s, sb_kv_s, q_n_s, kv_c_s, kv_s_s, kv_w_s, gates_s = _project(xs, g_attn_norm[l], w_in_b, cos_s, sin_s, tm=db)
    o_sb_s = _sb_decode(q_sb_s, cache_sb_kv[l], page_table)
    n_tok = past + dt
    n_cmp = n_tok // CMP_STRIDE - CMP_RATIO + 1
    n_sel = -(-n_tok // SEL_BLK)
    cmp_s = _compress_decode(cache_nsa_cmp_kv[l], page_table, cw)
    o_c_s, sel_s = _cmp_decode(q_n_s, cmp_s, past, n_cmp, n_sel)
    win_buf = state_nsa_win_kv[l]
    o_n_s = _sel_decode(q_n_s, sel_s, o_c_s, gates_s, kv_s_s, win_buf, kv_w_s, cache_nsa_slc_kv[l], page_table)
    y_sample = _out_mlp(xs, o_sb_s.reshape(db, -1), o_n_s.reshape(db, -1), g_sb_out[l], g_nsa_out[l], w_out_b,
                        g_mlp_norm[l], w_up_b, w_down_b, g_final, tm=db).reshape(db, dt, d)

    n_win_p = min(WINDOW, t)
    kv_w_p = kv_w.reshape(b, t, KV2)
    w_len = win_buf.shape[1]
    n_keep = min(WINDOW, w_len + dt)
    win_all = jnp.concatenate([win_buf.reshape(db, w_len, KV2), kv_w_s.reshape(db, dt, KV2)], axis=1)
    kvh = (2, NSA_KV_HEADS, HEAD_DIM)
    return (y_prompt, y_sample,
            sb_kv.reshape(1, b, t, 2, SB_HEADS, HEAD_DIM), sb_kv_s.reshape(1, db, dt, 2, SB_HEADS, HEAD_DIM),
            kv_c.reshape((1, b, t) + kvh), kv_c_s.reshape((1, db, dt) + kvh),
            kv_s.reshape((1, b, t) + kvh), kv_s_s.reshape((1, db, dt) + kvh),
            kv_w_p[:, t - n_win_p:].reshape((1, b, n_win_p) + kvh),
            win_all[:, w_len + dt - n_keep:].reshape((1, db, n_keep) + kvh))
```

```python
import functools
import math

import jax
import jax.numpy as jnp
import numpy as np
from jax import lax
from jax.experimental import pallas as pl
from jax.experimental.pallas import tpu as pltpu

F32 = jnp.float32
BF16 = jnp.bfloat16

HEAD_DIM = 64
SB_HEADS = 8
NSA_HEADS = 8
NSA_KV_HEADS = 2
NSA_GROUP = NSA_HEADS // NSA_KV_HEADS
SB_WIDTH = SB_HEADS * HEAD_DIM
NSA_WIDTH = NSA_HEADS * HEAD_DIM
KV_WIDTH = NSA_KV_HEADS * HEAD_DIM
KV2 = 2 * KV_WIDTH
N_GATES = 3 * NSA_HEADS
CMP_STRIDE = 16
CMP_RATIO = 2
CMP_BLK = CMP_STRIDE * CMP_RATIO
CMP_HID = 2 * HEAD_DIM
SEL_BLK = 64
TOP_N = 16
N_LOCAL = 2
WINDOW = 512
ROPE_THETA = 10000.0
EPS = 1e-6
NEG = -1e30
TINY = 1e-30
FORCE = 1e9
PAD_IMP = -3e38
LANES = 128
VMEM_LIMIT = 56 * 1024 * 1024
PAGES_PER_STEP = 8

_NT = (((1,), (1,)), ((), ()))


def _dot(a, b):
    return jnp.dot(a, b, preferred_element_type=F32)


def _dot_nt(a, b):
    return lax.dot_general(a, b, _NT, preferred_element_type=F32)


def _split_dot(x, m):
    hi = x.astype(BF16)
    lo = (x - hi.astype(F32)).astype(BF16)
    return _dot(hi, m) + _dot(lo, m)


def _rms(x, g):
    return x * lax.rsqrt(jnp.mean(x * x, axis=-1, keepdims=True) + EPS) * g


def _softplus(z):
    return jnp.maximum(z, 0.0) + jnp.log(1.0 + jnp.exp(-jnp.abs(z)))


def _iota(shape, axis):
    return lax.broadcasted_iota(jnp.int32, shape, axis)


def _params(*sem):
    return pltpu.CompilerParams(dimension_semantics=sem, vmem_limit_bytes=VMEM_LIMIT)


def _proj_kernel(x_ref, g_ref, w_ref, cos_ref, sin_ref,
                 qsb_ref, sbkv_ref, qn_ref, kvc_ref, kvs_ref, kvw_ref, gate_ref):
    h = _rms(x_ref[...], g_ref[...]).astype(BF16)
    cos = cos_ref[...]
    sin = sin_ref[...]

    def mm(lo, hi):
        return _dot(h, w_ref[:, lo:hi])

    def rope(z):
        w = z.shape[1]
        reps = w // LANES
        c = jnp.concatenate([cos] * reps, axis=1) if reps > 1 else cos
        s = jnp.concatenate([sin] * reps, axis=1) if reps > 1 else sin
        first = (_iota(z.shape, 1) % HEAD_DIM) < (HEAD_DIM // 2)
        rot = jnp.where(first, pltpu.roll(z, w - HEAD_DIM // 2, 1), pltpu.roll(z, HEAD_DIM // 2, 1))
        return z * c + rot * s

    o = 0
    qsb_ref[...] = mm(o, o + SB_WIDTH)
    o += SB_WIDTH
    sbkv_ref[...] = mm(o, o + 2 * SB_WIDTH)
    o += 2 * SB_WIDTH
    qn_ref[...] = rope(mm(o, o + NSA_WIDTH))
    o += NSA_WIDTH
    for ref in (kvc_ref, kvs_ref, kvw_ref):
        z = mm(o, o + KV2)
        ref[:, :KV_WIDTH] = rope(z[:, :KV_WIDTH])
        ref[:, KV_WIDTH:] = z[:, KV_WIDTH:]
        o += KV2
    g = mm(o, o + N_GATES)
    gate_ref[...] = 1.0 / (1.0 + jnp.exp(-g))


def _proj_prompt_kernel(x_ref, g_ref, w_ref, wt_ref, cos_ref, sin_ref, cost_ref, sint_ref,
                        qsb_ref, qn_ref, kvc_ref, gate_ref, sbkv_t_ref, kvc_t_ref, kvs_t_ref, kvw_t_ref):
    h = _rms(x_ref[...], g_ref[...]).astype(BF16)
    cos, sin = cos_ref[...], sin_ref[...]
    cos_t, sin_t = cost_ref[...], sint_ref[...]
    half = HEAD_DIM // 2

    def mm(lo, hi):
        return _dot(h, w_ref[:, lo:hi])

    def mm_t(lo, hi):
        return _dot_nt(wt_ref[lo:hi, :], h)

    def rope(z):
        w = z.shape[1]
        reps = w // LANES
        c = jnp.concatenate([cos] * reps, axis=1) if reps > 1 else cos
        s = jnp.concatenate([sin] * reps, axis=1) if reps > 1 else sin
        first = (_iota(z.shape, 1) % HEAD_DIM) < half
        rot = jnp.where(first, pltpu.roll(z, w - half, 1), pltpu.roll(z, half, 1))
        return z * c + rot * s

    def rope_t(zt):
        parts = []
        for r in range(0, zt.shape[0], HEAD_DIM):
            x1, x2 = zt[r:r + half], zt[r + half:r + HEAD_DIM]
            parts += [x1 * cos_t - x2 * sin_t, x2 * cos_t + x1 * sin_t]
        return jnp.concatenate(parts, axis=0)

    o_k = SB_WIDTH
    o_qn = o_k + 2 * SB_WIDTH
    o_c = o_qn + NSA_WIDTH
    qsb_ref[...] = mm(0, SB_WIDTH)
    qn_ref[...] = rope(mm(o_qn, o_qn + NSA_WIDTH))
    z = mm(o_c, o_c + KV2)
    kvc_ref[:, :KV_WIDTH] = rope(z[:, :KV_WIDTH])
    kvc_ref[:, KV_WIDTH:] = z[:, KV_WIDTH:]
    o_g = o_c + 3 * KV2
    gate_ref[...] = 1.0 / (1.0 + jnp.exp(-mm(o_g, o_g + N_GATES)))
    sbkv_t_ref[0] = mm_t(o_k, o_k + 2 * SB_WIDTH)
    for n, ref in enumerate((kvc_t_ref, kvs_t_ref, kvw_t_ref)):
        zt = mm_t(o_c + n * KV2, o_c + (n + 1) * KV2)
        ref[0, :KV_WIDTH, :] = rope_t(zt[:KV_WIDTH])
        ref[0, KV_WIDTH:, :] = zt[KV_WIDTH:]


def _rope_angles(pos):
    half = HEAD_DIM // 2
    inv = ROPE_THETA ** (-jnp.arange(half, dtype=F32) / half)
    ang = pos.astype(F32)[:, None] * inv[None, :]
    return jnp.cos(ang), jnp.sin(ang)


def _rope_tables(pos):
    cos, sin = _rope_angles(pos)
    reps = LANES // HEAD_DIM
    return (jnp.concatenate([cos, cos] * reps, axis=1), jnp.concatenate([-sin, sin] * reps, axis=1))


def _project_prompt(x, g, w_bf16, wt_bf16, tm):
    b, t, d = x.shape
    pos = jnp.arange(t)
    cos, sin = _rope_tables(pos)
    cos_t, sin_t = (a.T for a in _rope_angles(pos))
    half = HEAD_DIM // 2
    n_t = t // tm
    row = lambda i: (i, 0)
    tab = lambda i: (i % n_t, 0)
    tab_t = lambda i: (0, i % n_t)
    fixed = lambda i: (0, 0)
    out_t = lambda i: (i // n_t, 0, i % n_t)
    rows = ((SB_WIDTH, False), (NSA_WIDTH, False), (KV2, False), (N_GATES, False),
            (2 * SB_WIDTH, True), (KV2, True), (KV2, True), (KV2, True))
    return pl.pallas_call(
        _proj_prompt_kernel,
        grid=(b * n_t,),
        in_specs=[pl.BlockSpec((tm, d), row), pl.BlockSpec((1, d), fixed),
                  pl.BlockSpec(w_bf16.shape, fixed), pl.BlockSpec(wt_bf16.shape, fixed),
                  pl.BlockSpec((tm, LANES), tab), pl.BlockSpec((tm, LANES), tab),
                  pl.BlockSpec((half, tm), tab_t), pl.BlockSpec((half, tm), tab_t)],
        out_specs=[pl.BlockSpec((1, w, tm), out_t) if tr else pl.BlockSpec((tm, w), row) for w, tr in rows],
        out_shape=[jax.ShapeDtypeStruct((b, w, t) if tr else (b * t, w), F32) for w, tr in rows],
        compiler_params=_params("parallel"),
        name="proj_prompt",
    )(x.reshape(b * t, d), g.reshape(1, d), w_bf16, wt_bf16, cos, sin, cos_t, sin_t)


def _project(x2d, g, w_bf16, cos, sin, tm):
    n, d = x2d.shape
    n_tab = cos.shape[0] // tm
    row = lambda i: (i, 0)
    tab = lambda i: (i % n_tab, 0)
    fixed = lambda i: (0, 0)
    widths = (SB_WIDTH, 2 * SB_WIDTH, NSA_WIDTH, KV2, KV2, KV2, N_GATES)
    return pl.pallas_call(
        _proj_kernel,
        grid=(n // tm,),
        in_specs=[pl.BlockSpec((tm, d), row), pl.BlockSpec((1, d), fixed),
                  pl.BlockSpec(w_bf16.shape, fixed),
                  pl.BlockSpec((tm, LANES), tab), pl.BlockSpec((tm, LANES), tab)],
        out_specs=[pl.BlockSpec((tm, w), row) for w in widths],
        out_shape=[jax.ShapeDtypeStruct((n, w), F32) for w in widths],
        compiler_params=_params("parallel"),
        name="proj",
    )(x2d, g.reshape(1, d), w_bf16, cos, sin)


def _sbp_kernel(q_ref, k_ref, v_ref, o_ref, acc_ref, run_ref, *, blk):
    i = pl.program_id(2)
    lane = _iota((blk, LANES), 1)
    q = q_ref[0] * (HEAD_DIM ** -0.5)
    qh = [jnp.where(lane < HEAD_DIM, q, 0.0).astype(BF16), jnp.where(lane >= HEAD_DIM, q, 0.0).astype(BF16)]
    later = (_iota((blk, blk), 0) > _iota((blk, blk), 1)).astype(BF16)
    acc_ref[...] = jnp.zeros_like(acc_ref)
    run_ref[...] = jnp.zeros_like(run_ref)

    def step(j, mask):
        cols = pl.ds(pl.multiple_of(j * blk, blk), blk)
        kb = k_ref[0, :, cols].astype(BF16)
        vb = v_ref[0, :, cols].astype(BF16)
        for h in range(2):
            z = _dot(qh[h], kb)
            sp = _softplus(z)
            log_1m = -sp if mask is None else jnp.where(mask, -sp, 0.0)
            after = _split_dot(log_1m, later) + run_ref[h]
            a = jnp.exp((z - sp) + after)
            if mask is not None:
                a = jnp.where(mask, a, 0.0)
            acc_ref[h] += _dot_nt(a.astype(BF16), vb)
            run_ref[h] += jnp.sum(log_1m, axis=1, keepdims=True)

    step(i, _iota((blk, blk), 1) < _iota((blk, blk), 0))

    def body(jj, c):
        step(i - 1 - jj, None)
        return c

    lax.fori_loop(0, i, body, 0)
    o_ref[0] = jnp.where(lane < HEAD_DIM, acc_ref[0], acc_ref[1])


def _sb_prompt(q_sb, sb_kv_t, blk):
    b, t, _ = q_sb.shape
    pairs = SB_WIDTH // LANES
    return pl.pallas_call(
        functools.partial(_sbp_kernel, blk=blk),
        grid=(b, pairs, t // blk),
        in_specs=[pl.BlockSpec((1, blk, LANES), lambda bi, p, i: (bi, i, p)),
                  pl.BlockSpec((1, LANES, t), lambda bi, p, i: (bi, p, 0)),
                  pl.BlockSpec((1, LANES, t), lambda bi, p, i: (bi, pairs + p, 0))],
        out_specs=pl.BlockSpec((1, blk, LANES), lambda bi, p, i: (bi, i, p)),
        out_shape=jax.ShapeDtypeStruct((b, t, SB_WIDTH), F32),
        scratch_shapes=[pltpu.VMEM((2, blk, LANES), F32), pltpu.VMEM((2, blk, 1), F32)],
        compiler_params=_params("parallel", "parallel", "arbitrary"),
        name="sb_prompt",
    )(q_sb, sb_kv_t, sb_kv_t)


def _compress_weights(w_k1, w_k2, w_v1, w_v2, pe_k, pe_v):
    nkv = 2 * NSA_KV_HEADS
    eye = jnp.eye(nkv, dtype=F32)
    w1 = jnp.stack([w_k1] * NSA_KV_HEADS + [w_v1] * NSA_KV_HEADS)
    w1 = w1.reshape(nkv, CMP_RATIO, CMP_STRIDE, HEAD_DIM, CMP_HID)
    wab = jnp.einsum('shidn,st->itdhsn', w1, eye).reshape(CMP_STRIDE * nkv * HEAD_DIM, CMP_RATIO * nkv * CMP_HID)
    w2 = jnp.stack([w_k2] * NSA_KV_HEADS + [w_v2] * NSA_KV_HEADS)
    w2b = jnp.einsum('snd,st->sntd', w2, eye).reshape(nkv * CMP_HID, nkv * HEAD_DIM)
    pe = jnp.stack([pe_k] * NSA_KV_HEADS + [pe_v] * NSA_KV_HEADS)
    pe = pe.reshape(nkv, CMP_RATIO, CMP_STRIDE, HEAD_DIM).transpose(1, 2, 0, 3).reshape(CMP_RATIO, -1)
    return wab.astype(BF16), w2b.astype(BF16), pe


def _compress_tail(x, wab_ref, w2_ref, pe_ref, out_ref):
    n = x.shape[0]
    half = wab_ref.shape[1] // 2
    xa = (x + pe_ref[0:1, :]).astype(BF16)
    xb = (x + pe_ref[1:2, :]).astype(BF16)
    hid = _dot(xa, wab_ref[:, :half]) + pltpu.roll(_dot(xb, wab_ref[:, half:]), n - 1, 0)
    act = hid / (1.0 + jnp.exp(-hid))
    out = _dot(act.astype(BF16), w2_ref[...])
    for s in range(2 * NSA_KV_HEADS):
        out_ref[0, s] = out[:, s * HEAD_DIM:(s + 1) * HEAD_DIM]


def _compress_prompt_kernel(x_ref, wab_ref, w2_ref, pe_ref, out_ref):
    _compress_tail(x_ref[0], wab_ref, w2_ref, pe_ref, out_ref)


def _compress_prompt(kv_c, cw):
    wab, w2b, pe = cw
    b, t, _ = kv_c.shape
    n_chunks = t // CMP_STRIDE
    feat = CMP_STRIDE * KV2
    x = kv_c[:, :n_chunks * CMP_STRIDE].reshape(b, n_chunks, feat)
    fixed = lambda bi: (0, 0)
    return pl.pallas_call(
        _compress_prompt_kernel,
        grid=(b,),
        in_specs=[pl.BlockSpec((1, n_chunks, feat), lambda bi: (bi, 0, 0)),
                  pl.BlockSpec(wab.shape, fixed), pl.BlockSpec(w2b.shape, fixed), pl.BlockSpec(pe.shape, fixed)],
        out_specs=pl.BlockSpec((1, 2 * NSA_KV_HEADS, n_chunks, HEAD_DIM), lambda bi: (bi, 0, 0, 0)),
        out_shape=jax.ShapeDtypeStruct((b, 2 * NSA_KV_HEADS, n_chunks, HEAD_DIM), F32),
        compiler_params=_params("parallel"),
        name="compress_prompt",
    )(x, wab, w2b, pe)


def _compress_decode_kernel(pt_ref, *refs, rows):
    del pt_ref
    g = PAGES_PER_STEP
    pages, (wab_ref, w2_ref, pe_ref, out_ref, x_ref) = refs[:g], refs[g:]
    pg = pl.program_id(1)
    for k in range(g):
        x_ref[pl.ds((pg * g + k) * rows, rows), :] = pages[k][0]

    @pl.when(pg == pl.num_programs(1) - 1)
    def _():
        _compress_tail(x_ref[...], wab_ref, w2_ref, pe_ref, out_ref)


def _compress_decode(cache, page_table, cw):
    wab, w2b, pe = cw
    b, n_pages = page_table.shape
    n_phys, page = cache.shape[:2]
    rows = page // CMP_STRIDE
    feat = CMP_STRIDE * KV2
    x = cache.reshape(n_phys, rows, feat)
    g = PAGES_PER_STEP
    n_chunks = n_pages * rows
    fixed = lambda bi, pg, pt: (0, 0)
    page_specs = [pl.BlockSpec((1, rows, feat), functools.partial(lambda bi, pg, pt, k: (pt[bi, pg * g + k], 0, 0), k=k))
                  for k in range(g)]
    grid_spec = pltpu.PrefetchScalarGridSpec(
        num_scalar_prefetch=1,
        grid=(b, n_pages // g),
        in_specs=page_specs + [pl.BlockSpec(wab.shape, fixed), pl.BlockSpec(w2b.shape, fixed),
                               pl.BlockSpec(pe.shape, fixed)],
        out_specs=pl.BlockSpec((1, 2 * NSA_KV_HEADS, n_chunks, HEAD_DIM), lambda bi, pg, pt: (bi, 0, 0, 0)),
        scratch_shapes=[pltpu.VMEM((n_chunks, feat), F32)],
    )
    return pl.pallas_call(
        functools.partial(_compress_decode_kernel, rows=rows),
        grid_spec=grid_spec,
        out_shape=jax.ShapeDtypeStruct((b, 2 * NSA_KV_HEADS, n_chunks, HEAD_DIM), F32),
        compiler_params=_params("parallel", "arbitrary"),
        name="compress_decode",
    )(page_table, *([x] * g), wab, w2b, pe)


def _place(x, lane_off, width):
    parts = []
    if lane_off:
        parts.append(jnp.zeros((x.shape[0], lane_off), x.dtype))
    parts.append(x)
    rest = width - lane_off - x.shape[1]
    if rest:
        parts.append(jnp.zeros((x.shape[0], rest), x.dtype))
    return jnp.concatenate(parts, axis=1) if len(parts) > 1 else x


def _nsa_prompt_kernel(q_ref, cmp_ref, kvs_ref, kvw_ref, gate_ref, o_ref, m_ref, acc_ref, sel_ref,
                       *, blk, n_cmp, n_sel):
    i = pl.program_id(1)
    t_len = kvs_ref.shape[2]
    n_c = cmp_ref.shape[2]
    q_all = q_ref[0] * (HEAD_DIM ** -0.5)
    lane_kv = _iota((blk, KV2), 1)
    q_pos = i * blk + _iota((blk, 1), 0)
    gates = gate_ref[0]

    c_idx = _iota((blk, n_c), 1)
    cmp_ok = (c_idx * CMP_STRIDE + (CMP_BLK - 1) <= q_pos) & (c_idx < n_cmp)
    oc = _iota((n_c, LANES), 0) * CMP_STRIDE
    oj = _iota((n_c, LANES), 1) * SEL_BLK
    overlap = ((oc < oj + SEL_BLK) & (oc + CMP_BLK > oj)).astype(BF16)
    j_idx = _iota((blk, LANES), 1)
    cur = q_pos // SEL_BLK
    forced = (j_idx == 0) | ((j_idx <= cur) & (j_idx > cur - N_LOCAL))
    expand = (_iota((LANES, t_len), 0) == _iota((LANES, t_len), 1) // SEL_BLK).astype(BF16)
    blk_row = _iota((n_sel, blk), 0)
    on_key_rows = _iota((KV2, blk), 0) < KV_WIDTH

    def attend(qp, kv_ref, first_tile, n_tiles, bias_fn):
        m_ref[...] = jnp.full_like(m_ref, NEG)
        acc_ref[...] = jnp.zeros_like(acc_ref)

        def body(jj, c):
            jt = first_tile + jj
            kv_t = kv_ref[0, :, pl.ds(pl.multiple_of(jt * blk, blk), blk)].astype(BF16)
            s = _dot(qp, kv_t) + jnp.concatenate([bias_fn(jt)] * NSA_GROUP, axis=0)
            m_old = m_ref[...]
            m_new = jnp.maximum(m_old, jnp.max(s, axis=1, keepdims=True))
            p = jnp.exp(s - m_new).astype(BF16)
            acc_ref[...] = jnp.exp(m_old - m_new) * acc_ref[...] + _dot_nt(p, jnp.where(on_key_rows, 1.0, kv_t))
            m_ref[...] = m_new
            return c

        lax.fori_loop(0, n_tiles, body, 0)
        acc = acc_ref[...]
        return acc / jnp.maximum(acc[:, 0:1], TINY)

    o_cs, qps = [], []
    for kh in range(NSA_KV_HEADS):
        qk = q_all[:, kh * NSA_GROUP * HEAD_DIM:(kh + 1) * NSA_GROUP * HEAD_DIM]
        qk_bf = qk.astype(BF16)

        kc = cmp_ref[0, kh]
        vc = cmp_ref[0, NSA_KV_HEADS + kh]
        probs = []
        for g in range(NSA_GROUP):
            s = _dot_nt(qk_bf, _place(kc, g * HEAD_DIM, KV2).astype(BF16))
            s = jnp.where(cmp_ok, s, NEG)
            p = jnp.where(cmp_ok, jnp.exp(s - jnp.max(s, axis=1, keepdims=True)), 0.0)
            probs.append(p / jnp.maximum(jnp.sum(p, axis=1, keepdims=True), TINY))
        vc_rows = jnp.concatenate([_place(vc, g * HEAD_DIM, KV2) for g in range(NSA_GROUP)], axis=0)
        o_c = _dot(jnp.concatenate(probs, axis=1).astype(BF16), vc_rows.astype(BF16))
        p_sum = (probs[0] + probs[1]) + (probs[2] + probs[3])

        imp = _split_dot(p_sum, overlap)
        imp = jnp.where(forced, FORCE, imp)
        imp = jnp.where(j_idx * SEL_BLK <= q_pos, imp, NEG)
        imp = jnp.where(j_idx < n_sel, imp, PAD_IMP)
        imp_t = imp.T[:n_sel]
        rank = jnp.zeros((n_sel, blk), F32)
        for k in range(n_sel):
            row = imp_t[k:k + 1, :]
            ge = jnp.where(row >= imp_t, 1.0, 0.0)
            gt = jnp.where(row > imp_t, 1.0, 0.0)
            rank = rank + jnp.where(blk_row > k, ge, gt)
        sel_t = jnp.where(rank < float(min(TOP_N, n_sel)), 1.0, 0.0)
        sel = jnp.concatenate([sel_t, jnp.zeros((LANES - n_sel, blk), F32)], axis=0).T.astype(BF16)
        sel_ref[kh] = (_dot(sel, expand) - 1.0) * (-NEG)

        shift = [pltpu.roll(qk, (kh - g) * HEAD_DIM % KV2, 1) if (kh - g) % NSA_GROUP else qk for g in range(NSA_GROUP)]
        on_k = (lane_kv >= kh * HEAD_DIM) & (lane_kv < (kh + 1) * HEAD_DIM)
        qps.append(jnp.concatenate([jnp.where(on_k, x, 0.0) for x in shift], axis=0).astype(BF16))
        o_cs.append(o_c)

    out_parts = []
    for kh in range(NSA_KV_HEADS):
        o_c, qp = o_cs[kh], qps[kh]

        def sel_mask(jt, kh=kh):
            key = jt * blk + _iota((blk, blk), 1)
            return jnp.where(key <= q_pos, sel_ref[kh, :, pl.ds(pl.multiple_of(jt * blk, blk), blk)], NEG)

        def win_mask(jt):
            key = jt * blk + _iota((blk, blk), 1)
            return jnp.where((key <= q_pos) & (key > q_pos - WINDOW), 0.0, NEG)

        o_s = attend(qp, kvs_ref, 0, i + 1, sel_mask)
        w_tiles = -(-WINDOW // blk)
        first_w = jnp.maximum(i - w_tiles, 0)
        o_w = attend(qp, kvw_ref, first_w, i + 1 - first_w, win_mask)

        v_lo = KV_WIDTH + kh * HEAD_DIM
        res = jnp.zeros((blk, KV2), F32)
        for g in range(NSA_GROUP):
            head = kh * NSA_GROUP + g
            on_g = (lane_kv >= g * HEAD_DIM) & (lane_kv < (g + 1) * HEAD_DIM)
            sh = (g * HEAD_DIM - v_lo) % KV2
            osg = o_s[g * blk:(g + 1) * blk]
            owg = o_w[g * blk:(g + 1) * blk]
            mix = gates[:, 3 * head + 1:3 * head + 2] * osg + gates[:, 3 * head + 2:3 * head + 3] * owg
            mix = pltpu.roll(mix, sh, 1) if sh else mix
            res = res + jnp.where(on_g, mix + gates[:, 3 * head:3 * head + 1] * o_c, 0.0)
        out_parts.append(res)
    o_ref[0] = jnp.concatenate(out_parts, axis=1)


def _nsa_prompt(q_n, cmp_kv, kv_s_t, kv_w_t, gates, blk):
    b, t, _ = q_n.shape
    n_c = cmp_kv.shape[2]
    n_cmp = t // CMP_STRIDE - CMP_RATIO + 1
    n_sel = -(-t // SEL_BLK)
    rows = NSA_GROUP * blk
    return pl.pallas_call(
        functools.partial(_nsa_prompt_kernel, blk=blk, n_cmp=n_cmp, n_sel=n_sel),
        grid=(b, t // blk),
        in_specs=[pl.BlockSpec((1, blk, NSA_WIDTH), lambda bi, i: (bi, i, 0)),
                  pl.BlockSpec((1, 2 * NSA_KV_HEADS, n_c, HEAD_DIM), lambda bi, i: (bi, 0, 0, 0)),
                  pl.BlockSpec((1, KV2, t), lambda bi, i: (bi, 0, 0)),
                  pl.BlockSpec((1, KV2, t), lambda bi, i: (bi, 0, 0)),
                  pl.BlockSpec((1, blk, N_GATES), lambda bi, i: (bi, i, 0))],
        out_specs=pl.BlockSpec((1, blk, NSA_WIDTH), lambda bi, i: (bi, i, 0)),
        out_shape=jax.ShapeDtypeStruct((b, t, NSA_WIDTH), F32),
        scratch_shapes=[pltpu.VMEM((rows, 1), F32), pltpu.VMEM((rows, KV2), F32),
                        pltpu.VMEM((NSA_KV_HEADS, blk, t), F32)],
        compiler_params=_params("parallel", "arbitrary"),
        name="nsa_prompt",
    )(q_n, cmp_kv, kv_s_t, kv_w_t, gates)


def _out_mlp_kernel(x_ref, osb_ref, on_ref, gsb_ref, gn_ref, wo_ref, gm_ref, wu_ref, wd_ref, gf_ref, y_ref):
    a = _rms(osb_ref[...], gsb_ref[...]).astype(BF16)
    b = _rms(on_ref[...], gn_ref[...]).astype(BF16)
    x1 = x_ref[...] + _dot(a, wo_ref[:SB_WIDTH, :]) + _dot(b, wo_ref[SB_WIDTH:, :])
    h = _rms(x1, gm_ref[...]).astype(BF16)
    u = jnp.maximum(_dot(h, wu_ref[...]), 0.0)
    x2 = x1 + _dot((u * u).astype(BF16), wd_ref[...])
    y_ref[...] = _rms(x2, gf_ref[...])


def _out_mlp(x2d, o_sb, o_n, g_sb, g_nsa, w_out, g_mlp, w_up, w_down, g_final, tm):
    n, d = x2d.shape
    row = lambda i: (i, 0)
    fixed = lambda i: (0, 0)
    once = functools.partial(pl.BlockSpec, index_map=fixed)
    return pl.pallas_call(
        _out_mlp_kernel,
        grid=(n // tm,),
        in_specs=[pl.BlockSpec((tm, d), row), pl.BlockSpec((tm, SB_WIDTH), row), pl.BlockSpec((tm, NSA_WIDTH), row),
                  once((1, SB_WIDTH)), once((1, NSA_WIDTH)), once(w_out.shape), once((1, d)),
                  once(w_up.shape), once(w_down.shape), once((1, d))],
        out_specs=pl.BlockSpec((tm, d), row),
        out_shape=jax.ShapeDtypeStruct((n, d), F32),
        compiler_params=_params("parallel"),
        name="out_mlp",
    )(x2d, o_sb, o_n, g_sb.reshape(1, -1), g_nsa.reshape(1, -1), w_out, g_mlp.reshape(1, -1), w_up, w_down,
      g_final.reshape(1, -1))


def _token_minor(kv):
    n, tokens = kv.shape[:2]
    return jnp.transpose(kv, (0, 2, 3, 4, 1)).reshape(n, -1, tokens)


def _heads_on_lanes(x, width, group):
    reps = width // HEAD_DIM
    tiled = jnp.concatenate([x] * reps, axis=1)
    slot = _iota(tiled.shape, 1) // HEAD_DIM
    row = _iota(tiled.shape, 0)
    return jnp.where(slot == row // group, tiled, 0.0)


def _sb_decode_kernel(pt_ref, q_ref, *refs, page):
    del pt_ref
    g = PAGES_PER_STEP
    pages, (o_ref, acc_ref, run_ref) = refs[:g], refs[g:]
    pg = pl.program_id(1)

    @pl.when(pg == 0)
    def _():
        acc_ref[...] = jnp.zeros_like(acc_ref)
        run_ref[...] = jnp.zeros_like(run_ref)

    q = _heads_on_lanes(q_ref[0] * (HEAD_DIM ** -0.5), SB_WIDTH, 1).astype(BF16)
    later = (_iota((page, page), 0) > _iota((page, page), 1)).astype(BF16)
    k_t = jnp.concatenate([p[0, :SB_WIDTH, :].astype(BF16) for p in pages], axis=1)
    v_t = jnp.concatenate([p[0, SB_WIDTH:, :].astype(BF16) for p in pages], axis=1)
    z = _dot(q, k_t)
    sp = _softplus(z)
    log_1m = -sp
    chunks = [log_1m[:, c * page:(c + 1) * page] for c in range(g)]
    within = _split_dot(jnp.concatenate(chunks, axis=0), later)
    run = run_ref[...]
    after = [None] * g
    for c in reversed(range(g)):
        after[c] = within[c * SB_HEADS:(c + 1) * SB_HEADS] + run
        run = run + jnp.sum(chunks[c], axis=1, keepdims=True)
    run_ref[...] = run
    a = jnp.exp((z - sp) + jnp.concatenate(after, axis=1))
    acc_ref[...] += _dot_nt(a.astype(BF16), v_t)

    @pl.when(pg == pl.num_programs(1) - 1)
    def _():
        acc = acc_ref[...]
        row = _iota((SB_HEADS, HEAD_DIM), 0)
        out = jnp.zeros((SB_HEADS, HEAD_DIM), F32)
        for h in range(SB_HEADS):
            out = out + jnp.where(row == h, acc[:, h * HEAD_DIM:(h + 1) * HEAD_DIM], 0.0)
        o_ref[0] = out


def _sb_decode(q_sb, cache, page_table):
    b, n_pages = page_table.shape
    n_phys, page = cache.shape[:2]
    x = _token_minor(cache)
    g = PAGES_PER_STEP
    n_steps = n_pages // g

    def page_map(bi, pg, pt, k):
        return (pt[bi, (n_steps - 1 - pg) * g + k], 0, 0)

    grid_spec = pltpu.PrefetchScalarGridSpec(
        num_scalar_prefetch=1,
        grid=(b, n_steps),
        in_specs=[pl.BlockSpec((1, SB_HEADS, HEAD_DIM), lambda bi, pg, pt: (bi, 0, 0))]
        + [pl.BlockSpec((1, 2 * SB_WIDTH, page), functools.partial(page_map, k=k)) for k in range(g)],
        out_specs=pl.BlockSpec((1, SB_HEADS, HEAD_DIM), lambda bi, pg, pt: (bi, 0, 0)),
        scratch_shapes=[pltpu.VMEM((SB_HEADS, SB_WIDTH), F32), pltpu.VMEM((SB_HEADS, 1), F32)],
    )
    return pl.pallas_call(
        functools.partial(_sb_decode_kernel, page=page),
        grid_spec=grid_spec,
        out_shape=jax.ShapeDtypeStruct((b, SB_HEADS, HEAD_DIM), F32),
        compiler_params=_params("parallel", "arbitrary"),
        name="sb_decode",
    )(page_table, q_sb.reshape(b, SB_HEADS, HEAD_DIM), *([x] * g))


def _cmp_decode_kernel(q_ref, cmp_ref, oc_ref, sel_ref, *, n_cmp, n_sel, q_pos, sel_lanes):
    n_c = cmp_ref.shape[2]
    q = q_ref[0] * (HEAD_DIM ** -0.5)
    row = _iota((NSA_HEADS, 1), 0)
    c_idx = _iota((NSA_HEADS, n_c), 1)
    ok = (c_idx * CMP_STRIDE + (CMP_BLK - 1) <= q_pos) & (c_idx < n_cmp)
    s = jnp.zeros((NSA_HEADS, n_c), F32)
    for kh in range(NSA_KV_HEADS):
        s = s + jnp.where(row // NSA_GROUP == kh, _dot_nt(q.astype(BF16), cmp_ref[0, kh].astype(BF16)), 0.0)
    s = jnp.where(ok, s, NEG)
    p = jnp.where(ok, jnp.exp(s - jnp.max(s, axis=1, keepdims=True)), 0.0)
    p = p / jnp.maximum(jnp.sum(p, axis=1, keepdims=True), TINY)
    o_c = jnp.zeros((NSA_HEADS, HEAD_DIM), F32)
    for kh in range(NSA_KV_HEADS):
        o_c = o_c + jnp.where(row // NSA_GROUP == kh,
                              _dot(p.astype(BF16), cmp_ref[0, NSA_KV_HEADS + kh].astype(BF16)), 0.0)
    oc_ref[0] = o_c

    grp = jnp.zeros((NSA_HEADS, n_c), F32)
    for kh in range(NSA_KV_HEADS):
        tot = jnp.sum(jnp.where(row // NSA_GROUP == kh, p, 0.0), axis=0, keepdims=True)
        grp = grp + jnp.where(row // NSA_GROUP == kh, tot, 0.0)
    oc_ = _iota((n_c, sel_lanes), 0) * CMP_STRIDE
    oj = _iota((n_c, sel_lanes), 1) * SEL_BLK
    overlap = ((oc_ < oj + SEL_BLK) & (oc_ + CMP_BLK > oj)).astype(BF16)
    imp = _split_dot(grp, overlap)
    j_idx = _iota((NSA_HEADS, sel_lanes), 1)
    cur = q_pos // SEL_BLK
    forced = (j_idx == 0) | ((j_idx <= cur) & (j_idx > cur - N_LOCAL))
    imp = jnp.where(forced, FORCE, imp)
    imp = jnp.where(j_idx * SEL_BLK <= q_pos, imp, NEG)
    imp = jnp.where(j_idx < n_sel, imp, PAD_IMP)
    kk = _iota((sel_lanes, sel_lanes), 0)
    jj = _iota((sel_lanes, sel_lanes), 1)
    eye = kk == jj
    for kh in range(NSA_KV_HEADS):
        r0 = kh * NSA_GROUP
        imp_row = imp[r0:r0 + 1, :]
        imp_col = jnp.sum(jnp.where(eye, imp_row, 0.0), axis=1, keepdims=True)
        better = (imp_col > imp_row) | ((imp_col == imp_row) & (kk < jj))
        rank = jnp.sum(jnp.where(better, 1.0, 0.0), axis=0, keepdims=True)
        sel = jnp.where(rank < float(min(TOP_N, n_sel)), 1.0, 0.0)
        sel_ref[0, r0:r0 + NSA_GROUP, :] = jnp.broadcast_to(sel, (NSA_GROUP, sel_lanes))


def _cmp_decode(q_n, cmp_kv, q_pos, n_cmp, n_sel):
    b = q_n.shape[0]
    n_c = cmp_kv.shape[2]
    sel_lanes = -(-n_sel // LANES) * LANES
    return pl.pallas_call(
        functools.partial(_cmp_decode_kernel, n_cmp=n_cmp, n_sel=n_sel, q_pos=q_pos, sel_lanes=sel_lanes),
        grid=(b,),
        in_specs=[pl.BlockSpec((1, NSA_HEADS, HEAD_DIM), lambda bi: (bi, 0, 0)),
                  pl.BlockSpec((1, 2 * NSA_KV_HEADS, n_c, HEAD_DIM), lambda bi: (bi, 0, 0, 0))],
        out_specs=[pl.BlockSpec((1, NSA_HEADS, HEAD_DIM), lambda bi: (bi, 0, 0)),
                   pl.BlockSpec((1, NSA_HEADS, sel_lanes), lambda bi: (bi, 0, 0))],
        out_shape=[jax.ShapeDtypeStruct((b, NSA_HEADS, HEAD_DIM), F32),
                   jax.ShapeDtypeStruct((b, NSA_HEADS, sel_lanes), F32)],
        compiler_params=_params("parallel"),
        name="cmp_decode",
    )(q_n.reshape(b, NSA_HEADS, HEAD_DIM), cmp_kv)


def _sel_decode_kernel(pt_ref, q_ref, sel_ref, oc_ref, gate_ref, snew_ref, win_ref, wnew_ref, *refs,
                       page, past, win_len):
    del pt_ref
    g = PAGES_PER_STEP
    pages, (o_ref, m_ref, l_ref, acc_ref) = refs[:g], refs[g:]
    pg = pl.program_id(1)
    keys = g * page

    @pl.when(pg == 0)
    def _():
        m_ref[...] = jnp.full_like(m_ref, NEG)
        l_ref[...] = jnp.zeros_like(l_ref)
        acc_ref[...] = jnp.zeros_like(acc_ref)

    q = _heads_on_lanes(q_ref[0] * (HEAD_DIM ** -0.5), KV2, NSA_GROUP)
    q_bf = q.astype(BF16)
    sel = sel_ref[0]
    sel_lanes = sel.shape[1]

    def fold(s, ok, weigh):
        s = jnp.where(ok, s, NEG)
        m_new = jnp.maximum(m_ref[...], jnp.max(s, axis=1, keepdims=True))
        p = jnp.where(ok, jnp.exp(s - m_new), 0.0)
        alpha = jnp.exp(m_ref[...] - m_new)
        l_ref[...] = alpha * l_ref[...] + jnp.sum(p, axis=1, keepdims=True)
        acc_ref[...] = alpha * acc_ref[...] + weigh(p)
        m_ref[...] = m_new

    def fold_rows(kv_t, ok):
        kv_t = kv_t.astype(BF16)
        fold(_dot(q_bf, kv_t), ok, lambda p: _dot_nt(p.astype(BF16), kv_t))

    def fold_one(row, ok):
        row = jnp.broadcast_to(row, (NSA_HEADS, KV2))
        fold(jnp.sum(q * row, axis=1, keepdims=True), ok, lambda p: p * row)

    expand = (_iota((sel_lanes, keys), 0) == (pg * keys + _iota((sel_lanes, keys), 1)) // SEL_BLK).astype(BF16)
    fold_rows(jnp.concatenate([p[0] for p in pages], axis=1), _dot(sel.astype(BF16), expand) > 0.5)

    @pl.when(pg == pl.num_programs(1) - 1)
    def _():
        row = _iota((NSA_HEADS, HEAD_DIM), 0)

        def finish():
            o = acc_ref[...] / jnp.maximum(l_ref[...], TINY)
            out = jnp.zeros((NSA_HEADS, HEAD_DIM), F32)
            for kh in range(NSA_KV_HEADS):
                lo = KV_WIDTH + kh * HEAD_DIM
                out = out + jnp.where(row // NSA_GROUP == kh, o[:, lo:lo + HEAD_DIM], 0.0)
            return out

        new_blk = past // SEL_BLK
        fold_one(snew_ref[0], sel[:, new_blk:new_blk + 1] > 0.5)
        o_s = finish()

        m_ref[...] = jnp.full_like(m_ref, NEG)
        l_ref[...] = jnp.zeros_like(l_ref)
        acc_ref[...] = jnp.zeros_like(acc_ref)
        w_pos = past - win_len + _iota((NSA_HEADS, win_len), 1)
        fold_rows(win_ref[0], (w_pos <= past) & (w_pos > past - WINDOW) & (w_pos >= 0))
        fold_one(wnew_ref[0], jnp.full((NSA_HEADS, 1), True))
        o_w = finish()

        gates = gate_ref[0]
        o_ref[0] = gates[:, 0:1] * oc_ref[0] + gates[:, 1:2] * o_s + gates[:, 2:3] * o_w


def _sel_decode(q_n, sel, o_c, gates, kvs_new, win_buf, kvw_new, cache, page_table):
    b, n_pages = page_table.shape
    n_phys, page = cache.shape[:2]
    x = _token_minor(cache)
    g = PAGES_PER_STEP
    past = n_pages * page
    win_len = win_buf.shape[1]
    win_t = _token_minor(win_buf)
    sel_lanes = sel.shape[2]
    per_b = lambda bi, pg, pt: (bi, 0, 0)
    grid_spec = pltpu.PrefetchScalarGridSpec(
        num_scalar_prefetch=1,
        grid=(b, n_pages // g),
        in_specs=[pl.BlockSpec((1, NSA_HEADS, HEAD_DIM), per_b), pl.BlockSpec((1, NSA_HEADS, sel_lanes), per_b),
                  pl.BlockSpec((1, NSA_HEADS, HEAD_DIM), per_b), pl.BlockSpec((1, NSA_HEADS, 3), per_b),
                  pl.BlockSpec((1, 1, KV2), per_b), pl.BlockSpec((1, KV2, win_len), per_b),
                  pl.BlockSpec((1, 1, KV2), per_b)]
        + [pl.BlockSpec((1, KV2, page), functools.partial(lambda bi, pg, pt, k: (pt[bi, pg * g + k], 0, 0), k=k))
           for k in range(g)],
        out_specs=pl.BlockSpec((1, NSA_HEADS, HEAD_DIM), per_b),
        scratch_shapes=[pltpu.VMEM((NSA_HEADS, 1), F32), pltpu.VMEM((NSA_HEADS, 1), F32),
                        pltpu.VMEM((NSA_HEADS, KV2), F32)],
    )
    return pl.pallas_call(
        functools.partial(_sel_decode_kernel, page=page, past=past, win_len=win_len),
        grid_spec=grid_spec,
        out_shape=jax.ShapeDtypeStruct((b, NSA_HEADS, HEAD_DIM), F32),
        compiler_params=_params("parallel", "arbitrary"),
        name="sel_decode",
    )(page_table, q_n.reshape(b, NSA_HEADS, HEAD_DIM), sel, o_c, gates.reshape(b, NSA_HEADS, 3),
      kvs_new.reshape(b, 1, KV2), win_t, kvw_new.reshape(b, 1, KV2), *([x] * g))


def kernel(x_prompt, x_sample, cache_sb_kv, cache_nsa_cmp_kv, cache_nsa_slc_kv, state_nsa_win_kv, page_table, w_in, w_cmp_k1, w_cmp_k2, w_cmp_v1, w_cmp_v2, pe_cmp_k, pe_cmp_v, g_sb_out, g_nsa_out, w_out, g_attn_norm, g_mlp_norm, w_up, w_down, g_final):
    depth = w_in.shape[0]
    assert depth == 1, "single-layer step"
    b, t, d = x_prompt.shape
    db, dt, _ = x_sample.shape
    assert dt == 1, "decode kernels handle one new token per sequence"
    n_pages = page_table.shape[1]
    page = cache_sb_kv.shape[2]
    past = n_pages * page
    l = 0

    w_in_b = w_in[l].astype(BF16)
    w_out_b = w_out[l].astype(BF16)
    w_up_b = w_up[l].astype(BF16)
    w_down_b = w_down[l].astype(BF16)
    cw = _compress_weights(w_cmp_k1[l], w_cmp_k2[l], w_cmp_v1[l], w_cmp_v2[l], pe_cmp_k[l], pe_cmp_v[l])

    blk = 256
    xp = x_prompt.reshape(b * t, d)
    q_sb, q_n, kv_c, gates, sb_kv_t, kv_c_t, kv_s_t, kv_w_t = _project_prompt(
        x_prompt, g_attn_norm[l], w_in_b, w_in[l].T.astype(BF16), tm=512)
    o_sb = _sb_prompt(q_sb.reshape(b, t, -1), sb_kv_t, blk)
    cmp_p = _compress_prompt(kv_c.reshape(b, t, KV2), cw)
    o_n = _nsa_prompt(q_n.reshape(b, t, -1), cmp_p, kv_s_t, kv_w_t, gates.reshape(b, t, N_GATES), blk)
    y_prompt = _out_mlp(xp, o_sb.reshape(b * t, -1), o_n.reshape(b * t, -1), g_sb_out[l], g_nsa_out[l], w_out_b,
                        g_mlp_norm[l], w_up_b, w_down_b, g_final, tm=512).reshape(b, t, d)

    cos_s, sin_s = _rope_tables(jnp.full((db,), past))
    xs = x_sample.reshape(db, d)
    q_sb_s, sb_kv_s, q_n_s, kv_c_s, kv_s_s, kv_w_s, gates_s = _project(xs, g_attn_norm[l], w_in_b, cos_s, sin_s, tm=db)
    o_sb_s = _sb_decode(q_sb_s, cache_sb_kv[l], page_table)
    n_tok = past + dt
    n_cmp = n_tok // CMP_STRIDE - CMP_RATIO + 1
    n_sel = -(-n_tok // SEL_BLK)
    cmp_s = _compress_decode(cache_nsa_cmp_kv[l], page_table, cw)
    o_c_s, sel_s = _cmp_decode(q_n_s, cmp_s, past, n_cmp, n_sel)
    win_buf = state_nsa_win_kv[l]
    o_n_s = _sel_decode(q_n_s, sel_s, o_c_s, gates_s, kv_s_s, win_buf, kv_w_s, cache_nsa_slc_kv[l], page_table)
    y_sample = _out_mlp(xs, o_sb_s.reshape(db, -1), o_n_s.reshape(db, -1), g_sb_out[l], g_nsa_out[l], w_out_b,
                        g_mlp_norm[l], w_up_b, w_down_b, g_final, tm=db).reshape(db, dt, d)

    n_win_p = min(WINDOW, t)
    w_len = win_buf.shape[1]
    n_keep = min(WINDOW, w_len + dt)
    win_all = jnp.concatenate([win_buf.reshape(db, w_len, KV2), kv_w_s.reshape(db, dt, KV2)], axis=1)
    kvh = (2, NSA_KV_HEADS, HEAD_DIM)

    def rows_major(x_t, heads):
        n_tok = x_t.shape[2]
        return jnp.transpose(x_t.reshape(1, b, 2, heads, HEAD_DIM, n_tok), (0, 1, 5, 2, 3, 4))

    return (y_prompt, y_sample,
            rows_major(sb_kv_t, SB_HEADS), sb_kv_s.reshape(1, db, dt, 2, SB_HEADS, HEAD_DIM),
            rows_major(kv_c_t, NSA_KV_HEADS), kv_c_s.reshape((1, db, dt) + kvh),
            rows_major(kv_s_t, NSA_KV_HEADS), kv_s_s.reshape((1, db, dt) + kvh),
            rows_major(kv_w_t[:, :, t - n_win_p:], NSA_KV_HEADS),
            win_all[:, w_len + dt - n_keep:].reshape((1, db, n_keep) + kvh))
```
